```python
import jax
import jax.numpy as jnp
from jax import lax
import numpy as np

D_MODEL = 1024
BATCH = 4
SEQ = 8192
DEPTH = 2

GRID_W = 64
CTX_LEN = 256
EPS = 1e-6
N_MOD = 6

F_GROUPS = 4
F_GROUP_DIM = 64
F_WIDTH = F_GROUPS * F_GROUP_DIM
NA_HEADS = 8
NA_HEAD_DIM = 64
NA_WIDTH = NA_HEADS * NA_HEAD_DIM
WIN_H = 8
WIN_W = 16
Q_BLOCK = 16
KEY_SPAN = Q_BLOCK + WIN_W
CONV_GROUPS = 4
CONV_GROUP_DIM = 64
CONV_WIDTH = CONV_GROUPS * CONV_GROUP_DIM
CONV_K = 3
N_BRANCH = 3

Q_OFF = F_WIDTH
K_OFF = Q_OFF + NA_WIDTH
V_OFF = K_OFF + NA_WIDTH
U_OFF = V_OFF + NA_WIDTH
B_OFF = U_OFF + CONV_WIDTH
C_OFF = B_OFF + CONV_WIDTH
G_OFF = C_OFF + CONV_WIDTH
PROJ_WIDTH = G_OFF + N_BRANCH * D_MODEL
SPLITS = (Q_OFF, K_OFF, V_OFF, U_OFF, B_OFF, C_OFF, G_OFF)

D_FF = 2816
N_EXPERTS = 8
TOP_K = 2
D_FF_EXPERT = 3584
N_DENSE = (DEPTH + 1) // 2
N_MOE = DEPTH // 2

kernel_name = "hybrid_fnet_natten_shortconv_moe_dit"


def rmsnorm(x, g):
    xf = x.astype(jnp.float32)
    y = xf * lax.rsqrt(jnp.mean(xf * xf, axis=-1, keepdims=True) + EPS)
    return (y * g.astype(jnp.float32)).astype(x.dtype)


def modulate(h, shift, scale):
    return h * (1 + scale) + shift


def heads(t):
    return t.reshape(t.shape[0], t.shape[1], NA_HEADS, NA_HEAD_DIM)


def fourier_mix(u):
    b, l, _ = u.shape
    uf = u.astype(jnp.float32).reshape(b, l, F_GROUPS, F_GROUP_DIM)
    y = jnp.fft.fft2(uf, axes=(1, 3), norm="ortho").real
    return y.reshape(b, l, F_WIDTH).astype(u.dtype)


def short_conv(u, w):
    return lax.conv_general_dilated(
        u, w[:, None, :].astype(u.dtype), window_strides=(1,),
        padding=[(CONV_K // 2, CONV_K // 2)],
        dimension_numbers=("NWC", "WIO", "NWC"),
        feature_group_count=u.shape[-1])


def context_attention(q, k, v):
    s = jnp.einsum("blhd,bmhd->bhlm", q, k).astype(jnp.float32) * (NA_HEAD_DIM ** -0.5)
    p = jax.nn.softmax(s, axis=-1).astype(v.dtype)
    o = jnp.einsum("bhlm,bmhd->blhd", p, v)
    return o.reshape(o.shape[0], o.shape[1], NA_WIDTH)


def neighbourhood_attention(q, k, v, k_ctx, v_ctx, rpb):
    b, s_len, h, dh = q.shape
    rows = s_len // GRID_W
    kh = min(WIN_H, rows)
    qg = q.reshape(b, rows, GRID_W, h, dh)
    kg = k.reshape(b, rows, GRID_W, h, dh)
    vg = v.reshape(b, rows, GRID_W, h, dh)
    r = jnp.arange(rows)
    row_idx = jnp.clip(r - kh // 2, 0, rows - kh)[:, None] + jnp.arange(kh)[None, :]
    k_rows = kg[:, row_idx]
    v_rows = vg[:, row_idx]
    dr = row_idx - r[:, None] + (WIN_H - 1)
    col_start = np.clip(np.arange(GRID_W) - WIN_W // 2, 0, GRID_W - WIN_W)
    scale = NA_HEAD_DIM ** -0.5
    n_win = kh * KEY_SPAN
    outs = []
    for a in range(0, GRID_W, Q_BLOCK):
        s0 = min(max(a - WIN_W // 2, 0), GRID_W - KEY_SPAN)
        qc = np.arange(a, a + Q_BLOCK)
        kc = np.arange(s0, s0 + KEY_SPAN)
        cs = col_start[qc][:, None]
        in_win = (kc[None, :] >= cs) & (kc[None, :] < cs + WIN_W)
        dc = np.clip(kc[None, :] - qc[:, None] + (WIN_W - 1), 0, 2 * WIN_W - 2)
        qb = qg[:, :, a:a + Q_BLOCK]
        kb = k_rows[:, :, :, s0:s0 + KEY_SPAN]
        vb = v_rows[:, :, :, s0:s0 + KEY_SPAN]
        s_win = jnp.einsum("brqhd,brkwhd->bhrqkw", qb, kb).astype(jnp.float32) * scale
        bias = rpb[:, dr[:, None, :, None], dc[None, :, None, :]]
        s_win = s_win + bias.astype(jnp.float32)[None]
        s_win = jnp.where(in_win[:, None, :], s_win, -jnp.inf)
        s_ctx = jnp.einsum("brqhd,blhd->bhrql", qb, k_ctx).astype(jnp.float32) * scale
        s_all = jnp.concatenate([s_win.reshape(b, h, rows, Q_BLOCK, n_win), s_ctx], axis=-1)
        p = jax.nn.softmax(s_all, axis=-1).astype(v.dtype)
        p_win = p[..., :n_win].reshape(b, h, rows, Q_BLOCK, kh, KEY_SPAN)
        p_ctx = p[..., n_win:]
        o = (jnp.einsum("bhrqkw,brkwhd->brqhd", p_win, vb)
             + jnp.einsum("bhrql,blhd->brqhd", p_ctx, v_ctx))
        outs.append(o)
    out = jnp.concatenate(outs, axis=2)
    return out.reshape(b, s_len, NA_WIDTH)


def merge_branches(zf, zu, zb, zc, zg, attn, conv_w, w_fourier, w_na, w_conv_out, w_out):
    y_f = fourier_mix(zf) @ w_fourier
    y_a = attn @ w_na
    y_c = (zb * short_conv(zc * zu, conv_w)) @ w_conv_out
    g_f, g_a, g_c = jnp.split(jax.nn.sigmoid(zg), N_BRANCH, axis=-1)
    return (g_f * y_f + g_a * y_a + g_c * y_c) @ w_out


def swiglu(h, w_gate, w_up, w_down):
    return (jax.nn.silu(h @ w_gate) * (h @ w_up)) @ w_down


def moe(h, w_router, w_gate, w_up, w_down):
    logits = (h @ w_router).astype(jnp.float32)
    top_val, top_idx = lax.top_k(logits, TOP_K)
    top_w = jax.nn.softmax(top_val, axis=-1)
    gates = jnp.sum(jax.nn.one_hot(top_idx, N_EXPERTS, dtype=jnp.float32) * top_w[..., None],
                    axis=-2).astype(h.dtype)
    out = jnp.zeros_like(h)
    for e in range(N_EXPERTS):
        out = out + gates[..., e:e + 1] * swiglu(h, w_gate[e], w_up[e], w_down[e])
    return out


def setup_inputs(seed: int = 0) -> dict:
    key = jax.random.key(seed)
    ks = jax.random.split(key, 24)
    f32 = jnp.float32
    D = D_MODEL

    def nrm(k, shape, scale):
        return jax.random.normal(k, shape, f32) * scale

    return {
        "x": nrm(ks[0], (BATCH, SEQ, D), 1.0),
        "c": nrm(ks[1], (BATCH, D), 1.0),
        "ctx": nrm(ks[2], (BATCH, CTX_LEN, D), 1.0),
        "c_ctx": nrm(ks[3], (D,), 1.0),
        "norm1_g": 1.0 + nrm(ks[4], (DEPTH, D), 0.1),
        "norm2_g": 1.0 + nrm(ks[5], (DEPTH, D), 0.1),
        "w_ada": nrm(ks[6], (DEPTH, D, N_MOD * D), 0.5 * D ** -0.5),
        "b_ada": nrm(ks[7], (DEPTH, N_MOD * D), 0.01),
        "w_in": nrm(ks[8], (DEPTH, D, PROJ_WIDTH), D ** -0.5),
        "conv_w": nrm(ks[9], (DEPTH, CONV_K, CONV_WIDTH), CONV_K ** -0.5),
        "na_rpb": nrm(ks[10], (DEPTH, NA_HEADS, 2 * WIN_H - 1, 2 * WIN_W - 1), 0.1),
        "w_fourier": nrm(ks[11], (DEPTH, F_WIDTH, D), F_WIDTH ** -0.5),
        "w_na": nrm(ks[12], (DEPTH, NA_WIDTH, D), NA_WIDTH ** -0.5),
        "w_conv_out": nrm(ks[13], (DEPTH, CONV_WIDTH, D), CONV_WIDTH ** -0.5),
        "w_out": nrm(ks[14], (DEPTH, D, D), D ** -0.5),
        "ffn_w_gate": nrm(ks[15], (N_DENSE, D, D_FF), D ** -0.5),
        "ffn_w_up": nrm(ks[16], (N_DENSE, D, D_FF), D ** -0.5),
        "ffn_w_down": nrm(ks[17], (N_DENSE, D_FF, D), D_FF ** -0.5),
        "moe_router": nrm(ks[18], (N_MOE, D, N_EXPERTS), D ** -0.5),
        "moe_w_gate": nrm(ks[19], (N_MOE, N_EXPERTS, D, D_FF_EXPERT), D ** -0.5),
        "moe_w_up": nrm(ks[20], (N_MOE, N_EXPERTS, D, D_FF_EXPERT), D ** -0.5),
        "moe_w_down": nrm(ks[21], (N_MOE, N_EXPERTS, D_FF_EXPERT, D), D_FF_EXPERT ** -0.5),
        "final_g": 1.0 + nrm(ks[22], (D,), 0.1),
    }


def reference(x, c, ctx, c_ctx, norm1_g, norm2_g, w_ada, b_ada, w_in, conv_w, na_rpb,
              w_fourier, w_na, w_conv_out, w_out, ffn_w_gate, ffn_w_up, ffn_w_down,
              moe_router, moe_w_gate, moe_w_up, moe_w_down, final_g):
    def channel_mixer(l, h):
        if l % 2 == 0:
            j = l // 2
            return swiglu(h, ffn_w_gate[j], ffn_w_up[j], ffn_w_down[j])
        j = l // 2
        return moe(h, moe_router[j], moe_w_gate[j], moe_w_up[j], moe_w_down[j])

    for l in range(DEPTH):
        last = l == DEPTH - 1
        mod_x = [m[:, None, :] for m in jnp.split(jax.nn.silu(c) @ w_ada[l] + b_ada[l], N_MOD, axis=-1)]
        mod_c = jnp.split(jax.nn.silu(c_ctx) @ w_ada[l] + b_ada[l], N_MOD, axis=-1)

        hx = modulate(rmsnorm(x, norm1_g[l]), mod_x[0], mod_x[1])
        hc = modulate(rmsnorm(ctx, norm1_g[l]), mod_c[0], mod_c[1])
        zf, zq, zk, zv, zu, zb, zcg, zg = jnp.split(hx @ w_in[l], SPLITS, axis=-1)

        if last:
            k_c, v_c = jnp.split(hc @ w_in[l][:, K_OFF:U_OFF], 2, axis=-1)
            k_c, v_c = heads(k_c), heads(v_c)
        else:
            cf, cq, ck, cv, cu, cb, ccg, cg = jnp.split(hc @ w_in[l], SPLITS, axis=-1)
            k_c, v_c = heads(ck), heads(cv)
            attn_c = context_attention(heads(cq), k_c, v_c)
            mix_c = merge_branches(cf, cu, cb, ccg, cg, attn_c, conv_w[l],
                                   w_fourier[l], w_na[l], w_conv_out[l], w_out[l])
            ctx_mid = ctx + mod_c[2] * mix_c
            hc2 = modulate(rmsnorm(ctx_mid, norm2_g[l]), mod_c[3], mod_c[4])
            ctx = ctx_mid + mod_c[5] * channel_mixer(l, hc2)

        attn_x = neighbourhood_attention(heads(zq), heads(zk), heads(zv), k_c, v_c, na_rpb[l])
        mix_x = merge_branches(zf, zu, zb, zcg, zg, attn_x, conv_w[l],
                               w_fourier[l], w_na[l], w_conv_out[l], w_out[l])
        x = x + mod_x[2] * mix_x

        hx2 = modulate(rmsnorm(x, norm2_g[l]), mod_x[3], mod_x[4])
        x = x + mod_x[5] * channel_mixer(l, hx2)

    return rmsnorm(x, final_g)
```

```python
import functools

import numpy as np
import jax
import jax.numpy as jnp
from jax import lax
from jax.experimental import pallas as pl
from jax.experimental.pallas import tpu as pltpu

F32 = jnp.float32
BF16 = jnp.bfloat16
I32 = jnp.int32

EPS = 1e-6
N_MOD = 6
GRID_W = 64
WIN_H = 8
WIN_W = 16
F_GROUPS = 4
F_GROUP_DIM = 64
F_WIDTH = F_GROUPS * F_GROUP_DIM
NA_HEADS = 8
NA_HEAD_DIM = 64
NA_WIDTH = NA_HEADS * NA_HEAD_DIM
CONV_WIDTH = 256
CONV_K = 3
N_BRANCH = 3
TOP_K = 2
Q_OFF = F_WIDTH
K_OFF = Q_OFF + NA_WIDTH
V_OFF = K_OFF + NA_WIDTH
U_OFF = V_OFF + NA_WIDTH
B_OFF = U_OFF + CONV_WIDTH
C_OFF = B_OFF + CONV_WIDTH
G_OFF = C_OFF + CONV_WIDTH

LANES = 128
BF16_SUBLANES = 16
VMEM_LIMIT_BYTES = 56 * 1024 * 1024

ROW_TILE = 512
Q_ROWS = 8
KEY_ROWS = 16
MASK_VALUE = -1e30
MOE_EXPERT_TILE = 512
MOE_SLOT_TILE = 256
MOE_TOKEN_CHUNK = 512
MOE_FF_TILE = 512


def _params(sem):
    return pltpu.CompilerParams(dimension_semantics=sem, vmem_limit_bytes=VMEM_LIMIT_BYTES)


def _const_spec(shape):
    nd = len(shape)
    return pl.BlockSpec(shape, lambda *_: (0,) * nd, pipeline_mode=pl.Buffered(1))


def _dot(a, b):
    return jnp.dot(a, b, preferred_element_type=F32)


def _split_bf16(x):
    hi = x.astype(BF16)
    lo = (x - hi.astype(F32)).astype(BF16)
    return hi, lo


def _dot_split(a, w):
    ah, al = _split_bf16(a)
    wh, wl = _split_bf16(w)
    return _dot(ah, wh) + (_dot(ah, wl) + _dot(al, wh))


def _norm_mod(x, g, shift, scale):
    ms = jnp.mean(x * x, axis=-1, keepdims=True)
    y = x * lax.rsqrt(ms + EPS) * g
    return y * (1.0 + scale) + shift


def _rmsnorm(x, g):
    ms = jnp.mean(x * x, axis=-1, keepdims=True)
    return x * lax.rsqrt(ms + EPS) * g


def _adaln_kernel(c_ref, w_ref, b_ref, o_ref):
    a = c_ref[...]
    a = a * jax.nn.sigmoid(a)
    o_ref[0] = _dot_split(a, w_ref[0]) + b_ref[0]


def _adaln(cvec, w_ada, b_ada):
    depth, d, width = w_ada.shape
    rows = cvec.shape[0]
    tn = width // 4
    return pl.pallas_call(
        _adaln_kernel,
        grid=(depth, width // tn),
        in_specs=[
            pl.BlockSpec((rows, d), lambda l, j: (0, 0)),
            pl.BlockSpec((1, d, tn), lambda l, j: (l, 0, j)),
            pl.BlockSpec((1, 1, tn), lambda l, j: (l, 0, j)),
        ],
        out_specs=pl.BlockSpec((1, rows, tn), lambda l, j: (l, 0, j)),
        out_shape=jax.ShapeDtypeStruct((depth, rows, width), F32),
        compiler_params=_params(("arbitrary", "arbitrary")),
        name="adaln",
    )(cvec, w_ada, b_ada.reshape(depth, 1, width))


def _inproj_kernel(x_ref, mod_ref, g_ref, w_ref, bd_ref,
                   a0_ref, b0_ref, q_ref, k_ref, v_ref, u_ref, b_ref, c_ref, gate_ref, *, d):
    h = _norm_mod(x_ref[...], g_ref[...], mod_ref[0, 0:1, :], mod_ref[0, 1:2, :]).astype(BF16)

    def proj(c0, c1):
        return _dot(h, w_ref[:, c0:c1])

    zf = proj(0, Q_OFF).astype(BF16)
    ab = _dot(zf, bd_ref[...])
    a0_ref[...] = ab[:, :F_WIDTH].astype(BF16)
    b0_ref[...] = ab[:, F_WIDTH:].astype(BF16)
    q_ref[...] = (proj(Q_OFF, K_OFF) * (NA_HEAD_DIM ** -0.5)).astype(BF16)
    k_ref[...] = proj(K_OFF, V_OFF).astype(BF16)
    v_ref[...] = proj(V_OFF, U_OFF).astype(BF16)
    u_ref[...] = proj(U_OFF, B_OFF).astype(BF16)
    b_ref[...] = proj(B_OFF, C_OFF).astype(BF16)
    c_ref[...] = proj(C_OFF, G_OFF).astype(BF16)
    step = 512 if d % 512 == 0 else d
    for j in range(0, N_BRANCH * d, step):
        gate_ref[:, j:j + step] = jax.nn.sigmoid(proj(G_OFF + j, G_OFF + j + step)).astype(BF16)


def _inproj(x2, mod, g, w_bf, bd, seq_len):
    n, d = x2.shape
    tm = min(ROW_TILE, seq_len)
    pw = w_bf.shape[1]
    widths = [F_WIDTH, F_WIDTH, NA_WIDTH, NA_WIDTH, NA_WIDTH, CONV_WIDTH, CONV_WIDTH, CONV_WIDTH, N_BRANCH * d]
    row = lambda w: pl.BlockSpec((tm, w), lambda i: (i, 0))
    return pl.pallas_call(
        functools.partial(_inproj_kernel, d=d),
        grid=(n // tm,),
        in_specs=[
            row(d),
            pl.BlockSpec((1, N_MOD, d), lambda i: (i * tm // seq_len, 0, 0)),
            _const_spec((1, d)),
            _const_spec((d, pw)),
            _const_spec((F_WIDTH, 2 * F_WIDTH)),
        ],
        out_specs=[row(w) for w in widths],
        out_shape=[jax.ShapeDtypeStruct((n, w), BF16) for w in widths],
        compiler_params=_params(("parallel",)),
        name="inproj",
    )(x2, mod, g.reshape(1, d), w_bf, bd)


def _fft_split(seq_len):
    b = 64 if seq_len >= 1024 else 16
    return seq_len // b, b


def _fft_tables(seq_len):
    a_pts, b_pts = _fft_split(seq_len)
    ia = np.arange(a_pts)
    ang = 2.0 * np.pi * ((ia[:, None] * ia[None, :]) % a_pts) / a_pts
    c, s = np.cos(ang) / np.sqrt(a_pts), np.sin(ang) / np.sqrt(a_pts)
    w1 = np.block([[c, -s], [-s, -c]])
    ib = np.arange(b_pts)
    num = (ib[None, :, None] * ib[None, None, :] * a_pts + ib[None, None, :] * ia[:, None, None]) % seq_len
    th = 2.0 * np.pi * num / seq_len
    mc, ms = np.cos(th) / np.sqrt(b_pts), np.sin(th) / np.sqrt(b_pts)
    ic = np.arange(F_GROUP_DIM)
    angc = 2.0 * np.pi * ((ic[:, None] * ic[None, :]) % F_GROUP_DIM) / F_GROUP_DIM
    eye = np.eye(F_GROUPS)
    bd = np.concatenate([np.kron(eye, np.cos(angc)), np.kron(eye, np.sin(angc))], axis=1) / np.sqrt(F_GROUP_DIM)
    return tuple(jnp.asarray(t, F32).astype(BF16) for t in (w1, mc, ms, bd))


def _fft1_kernel(a_ref, b_ref, w_ref, o_ref, *, a_pts):
    x = jnp.concatenate([a_ref[0], b_ref[0]], axis=0)
    t = _dot(w_ref[...], x)
    o_ref[0, 0] = t[:a_pts].astype(BF16)
    o_ref[0, 1] = t[a_pts:].astype(BF16)


def _fft2_kernel(tr_ref, ti_ref, mc_ref, ms_ref, o_ref, *, ac):
    for j in range(ac):
        y = _dot(mc_ref[j], tr_ref[0, 0, j]) + _dot(ms_ref[j], ti_ref[0, 0, j])
        o_ref[0, j] = y.astype(BF16)


def _fourier(a0, b0, bt, seq_len, w1, mc, ms):
    a_pts, b_pts = _fft_split(seq_len)
    c = F_WIDTH
    bc = b_pts * c
    cw = min(2048, bc)
    view = lambda z: z.reshape(bt, a_pts, bc)
    t = pl.pallas_call(
        functools.partial(_fft1_kernel, a_pts=a_pts),
        grid=(bt, bc // cw),
        in_specs=[
            pl.BlockSpec((1, a_pts, cw), lambda b, j: (b, 0, j)),
            pl.BlockSpec((1, a_pts, cw), lambda b, j: (b, 0, j)),
            _const_spec((2 * a_pts, 2 * a_pts)),
        ],
        out_specs=pl.BlockSpec((1, 2, a_pts, cw), lambda b, j: (b, 0, 0, j)),
        out_shape=jax.ShapeDtypeStruct((bt, 2, a_pts, bc), BF16),
        compiler_params=_params(("parallel", "parallel")),
        name="fft1",
    )(view(a0), view(b0), w1)
    t5 = t.reshape(bt, 2, a_pts, b_pts, c)
    ac = min(a_pts, 16)
    out = pl.pallas_call(
        functools.partial(_fft2_kernel, ac=ac),
        grid=(a_pts // ac, bt),
        in_specs=[
            pl.BlockSpec((1, 1, ac, b_pts, c), lambda i, b: (b, 0, i, 0, 0)),
            pl.BlockSpec((1, 1, ac, b_pts, c), lambda i, b: (b, 1, i, 0, 0)),
            pl.BlockSpec((ac, b_pts, b_pts), lambda i, b: (i, 0, 0)),
            pl.BlockSpec((ac, b_pts, b_pts), lambda i, b: (i, 0, 0)),
        ],
        out_specs=pl.BlockSpec((1, ac, b_pts, c), lambda i, b: (b, i, 0, 0)),
        out_shape=jax.ShapeDtypeStruct((bt, a_pts, b_pts, c), BF16),
        compiler_params=_params(("parallel", "parallel")),
        name="fft2",
    )(t5, t5, mc, ms)
    return out.reshape(bt * a_pts, bc)


def _bias_tables(rpb, rows):
    w = GRID_W
    r0s = np.array([0, Q_ROWS, rows - Q_ROWS])
    i = np.arange(Q_ROWS)
    j = np.arange(KEY_ROWS)
    r = r0s[:, None, None] + i[None, :, None]
    kr = r0s[:, None, None] - WIN_H // 2 + j[None, None, :]
    base = np.clip(r - WIN_H // 2, 0, rows - WIN_H)
    ok_r = (kr >= 0) & (kr < rows) & (kr >= base) & (kr < base + WIN_H)
    dr = np.clip(kr - r + (WIN_H - 1), 0, 2 * WIN_H - 2)
    cq = np.arange(w)[:, None]
    kc = np.arange(w)[None, :]
    cs = np.clip(cq - WIN_W // 2, 0, w - WIN_W)
    ok_c = (kc >= cs) & (kc < cs + WIN_W)
    dc = np.clip(kc - cq + (WIN_W - 1), 0, 2 * WIN_W - 2)
    g = rpb.astype(F32)[:, dr[:, :, None, :, None], dc[None, None, :, None, :]]
    ok = ok_r[:, :, None, :, None] & ok_c[None, None, :, None, :]
    g = jnp.where(ok[None], g, MASK_VALUE)
    return jnp.transpose(g, (1, 0, 2, 3, 4, 5)).reshape(3, NA_HEADS, Q_ROWS * w, KEY_ROWS * w)


def _softmax_pv(s_parts, v_parts):
    m = s_parts[0].max(axis=-1, keepdims=True)
    for s in s_parts[1:]:
        m = jnp.maximum(m, s.max(axis=-1, keepdims=True))
    l = None
    pv = None
    for s, v in zip(s_parts, v_parts):
        p = jnp.exp(s - m)
        ls = p.sum(axis=-1, keepdims=True)
        o = _dot(p.astype(BF16), v)
        l = ls if l is None else l + ls
        pv = o if pv is None else pv + o
    return pv * (1.0 / l)


def _natten_kernel(q_ref, kp_ref, kc_ref, kn_ref, vp_ref, vc_ref, vn_ref, kx_ref, vx_ref, bias_ref,
                   o_ref, kw_ref, vw_ref, *, lc):
    qb = Q_ROWS * GRID_W
    halo = (KEY_ROWS - Q_ROWS) // 2 * GRID_W
    wk = KEY_ROWS * GRID_W
    for dst, (p, c, n, x) in ((kw_ref, (kp_ref, kc_ref, kn_ref, kx_ref)), (vw_ref, (vp_ref, vc_ref, vn_ref, vx_ref))):
        dst[0:halo] = p[qb - halo:qb]
        dst[halo:halo + qb] = c[...]
        dst[halo + qb:wk] = n[0:halo]
        dst[wk:wk + lc] = x[...]
    lane = lax.broadcasted_iota(I32, (qb, LANES), 1)
    low = lane < NA_HEAD_DIM
    for hp in range(NA_HEADS // 2):
        sl = slice(LANES * hp, LANES * (hp + 1))
        qp = q_ref[:, sl]
        kw = kw_ref[:, sl]
        vw = vw_ref[:, sl]
        outs = []
        for sub in range(2):
            qm = jnp.where(low if sub == 0 else jnp.logical_not(low), qp, jnp.zeros_like(qp))
            s = lax.dot_general(qm, kw, (((1,), (1,)), ((), ())), preferred_element_type=F32)
            s_win = s[:, :wk] + bias_ref[0, 2 * hp + sub]
            outs.append(_softmax_pv([s_win, s[:, wk:]], [vw[:wk], vw[wk:]]))
        o_ref[:, sl] = jnp.where(low, outs[0], outs[1]).astype(BF16)


def _natten(q, k, v, kx, vx, bias, bt, seq_len, lc):
    n = q.shape[0]
    qb = Q_ROWS * GRID_W
    nb = seq_len // qb
    assert seq_len % qb == 0 and nb >= 2, "neighbourhood attention kernel needs >= 16 grid rows, a multiple of 8"
    wk = KEY_ROWS * GRID_W
    cur = lambda b, r: (b * nb + r, 0)
    prev = lambda b, r: (b * nb + jnp.maximum(r - 1, 0), 0)
    nxt = lambda b, r: (b * nb + jnp.minimum(r + 1, nb - 1), 0)
    blk = lambda im: pl.BlockSpec((qb, NA_WIDTH), im)
    variant = lambda b, r: (jnp.where(r == 0, 0, jnp.where(r == nb - 1, 2, 1)), 0, 0, 0)
    return pl.pallas_call(
        functools.partial(_natten_kernel, lc=lc),
        grid=(bt, nb),
        in_specs=[
            blk(cur), blk(prev), blk(cur), blk(nxt), blk(prev), blk(cur), blk(nxt),
            pl.BlockSpec((lc, NA_WIDTH), lambda b, r: (b, 0)),
            pl.BlockSpec((lc, NA_WIDTH), lambda b, r: (b, 0)),
            pl.BlockSpec((1, NA_HEADS, qb, wk), variant, pipeline_mode=pl.Buffered(1)),
        ],
        out_specs=blk(cur),
        out_shape=jax.ShapeDtypeStruct((n, NA_WIDTH), BF16),
        scratch_shapes=[pltpu.VMEM((wk + lc, NA_WIDTH), BF16), pltpu.VMEM((wk + lc, NA_WIDTH), BF16)],
        compiler_params=_params(("parallel", "arbitrary")),
        name="natten",
    )(q, k, k, k, v, v, v, kx, vx, bias)


def _ctxattn_kernel(q_ref, k_ref, v_ref, o_ref):
    lc = q_ref.shape[0]
    lane = lax.broadcasted_iota(I32, (lc, LANES), 1)
    low = lane < NA_HEAD_DIM
    for hp in range(NA_HEADS // 2):
        sl = slice(LANES * hp, LANES * (hp + 1))
        qp = q_ref[:, sl]
        kw = k_ref[:, sl]
        vw = v_ref[:, sl]
        outs = []
        for sub in range(2):
            qm = jnp.where(low if sub == 0 else jnp.logical_not(low), qp, jnp.zeros_like(qp))
            s = lax.dot_general(qm, kw, (((1,), (1,)), ((), ())), preferred_element_type=F32)
            outs.append(_softmax_pv([s], [vw]))
        o_ref[:, sl] = jnp.where(low, outs[0], outs[1]).astype(BF16)


def _ctxattn(q, k, v, bt, lc):
    blk = pl.BlockSpec((lc, NA_WIDTH), lambda b: (b, 0))
    return pl.pallas_call(
        _ctxattn_kernel,
        grid=(bt,),
        in_specs=[blk, blk, blk],
        out_specs=blk,
        out_shape=jax.ShapeDtypeStruct(q.shape, BF16),
        compiler_params=_params(("parallel",)),
        name="ctxattn",
    )(q, k, v)


def _merge_kernel(fm_ref, at_ref, u_ref, b_ref, c_ref, up_ref, cp_ref, un_ref, cn_ref, gate_ref, x_ref, mod_ref,
                  cw_ref, wf_ref, wa_ref, wc_ref, wo_ref, o_ref, pre_ref, *, tm, d, a_pts, seq_len):
    i = pl.program_id(0)
    t0 = i * tm
    first = (t0 % seq_len) == 0
    last = ((t0 + tm) % seq_len) == 0
    p = c_ref[...].astype(F32) * u_ref[...].astype(F32)
    hs = BF16_SUBLANES
    p_above = cp_ref[hs - 1:hs, :].astype(F32) * up_ref[hs - 1:hs, :].astype(F32)
    p_below = cn_ref[0:1, :].astype(F32) * un_ref[0:1, :].astype(F32)
    p_above = jnp.where(first, 0.0, p_above)
    p_below = jnp.where(last, 0.0, p_below)
    row = lax.broadcasted_iota(I32, (tm, CONV_WIDTH), 0)
    p_dn = jnp.where(row == 0, p_above, pltpu.roll(p, 1, 0))
    p_up = jnp.where(row == tm - 1, p_below, pltpu.roll(p, tm - 1, 0))
    cv = cw_ref[0:1, :] * p_dn + cw_ref[1:2, :] * p + cw_ref[2:3, :] * p_up
    y_c = _dot((b_ref[...].astype(F32) * cv).astype(BF16), wc_ref[...])
    y_a = _dot(at_ref[...], wa_ref[...])
    pre_ref[...] = (gate_ref[:, d:2 * d].astype(F32) * y_a + gate_ref[:, 2 * d:3 * d].astype(F32) * y_c)
    for j in range(tm // a_pts):
        rs = slice(j * a_pts, (j + 1) * a_pts)
        y_f = _dot(fm_ref[:, j * F_WIDTH:(j + 1) * F_WIDTH], wf_ref[...])
        pre_ref[rs, :] += gate_ref[rs, 0:d].astype(F32) * y_f
    mix = _dot(pre_ref[...].astype(BF16), wo_ref[...])
    o_ref[...] = x_ref[...] + mod_ref[0, 2:3, :] * mix


def _merge(fm, attn, zu, zb, zc, gates, x2, mod, conv_w, wf, wa, wc, wo, seq_len):
    n, d = x2.shape
    tm = min(ROW_TILE, seq_len)
    a_pts, b_pts = _fft_split(seq_len)
    assert tm % a_pts == 0
    hs = BF16_SUBLANES
    nh = n // hs
    per = tm // hs
    row = lambda w: pl.BlockSpec((tm, w), lambda i: (i, 0))
    above = pl.BlockSpec((hs, CONV_WIDTH), lambda i: (jnp.maximum(i * per - 1, 0), 0))
    below = pl.BlockSpec((hs, CONV_WIDTH), lambda i: (jnp.minimum((i + 1) * per, nh - 1), 0))
    tiles_per_seq = seq_len // tm
    sub = tm // a_pts
    fm_spec = pl.BlockSpec((a_pts, sub * F_WIDTH), lambda i: (i // tiles_per_seq, i % tiles_per_seq))
    return pl.pallas_call(
        functools.partial(_merge_kernel, tm=tm, d=d, a_pts=a_pts, seq_len=seq_len),
        grid=(n // tm,),
        in_specs=[
            fm_spec, row(NA_WIDTH), row(CONV_WIDTH), row(CONV_WIDTH), row(CONV_WIDTH),
            above, above, below, below,
            row(N_BRANCH * d), row(d),
            pl.BlockSpec((1, N_MOD, d), lambda i: (i * tm // seq_len, 0, 0)),
            _const_spec((CONV_K, CONV_WIDTH)),
            _const_spec((F_WIDTH, d)), _const_spec((NA_WIDTH, d)), _const_spec((CONV_WIDTH, d)), _const_spec((d, d)),
        ],
        out_specs=row(d),
        out_shape=jax.ShapeDtypeStruct((n, d), F32),
        scratch_shapes=[pltpu.VMEM((tm, d), F32)],
        compiler_params=_params(("parallel",)),
        name="merge",
    )(fm, attn, zu, zb, zc, zu, zc, zu, zc, gates, x2, mod, conv_w, wf, wa, wc, wo)


def _ffn_kernel(x_ref, mod_ref, g_ref, wg_ref, wu_ref, wd_ref, fg_ref, o_ref, acc_ref, *, tf, final):
    x = x_ref[...]
    h = _norm_mod(x, g_ref[...], mod_ref[0, 3:4, :], mod_ref[0, 4:5, :]).astype(BF16)
    ff = wg_ref.shape[1]
    for f in range(0, ff, tf):
        g = _dot(h, wg_ref[:, f:f + tf])
        u = _dot(h, wu_ref[:, f:f + tf])
        a = (g * jax.nn.sigmoid(g) * u).astype(BF16)
        y = _dot(a, wd_ref[f:f + tf, :])
        if f == 0:
            acc_ref[...] = y
        else:
            acc_ref[...] += y
    out = x + mod_ref[0, 5:6, :] * acc_ref[...]
    o_ref[...] = _rmsnorm(out, fg_ref[...]) if final else out


def _ffn(x2, mod, g, wg, wu, wd, final_g, seq_len, final):
    n, d = x2.shape
    ff = wg.shape[1]
    tm = min(ROW_TILE, seq_len)
    tf = 256 if ff % 256 == 0 else LANES
    row = pl.BlockSpec((tm, d), lambda i: (i, 0))
    return pl.pallas_call(
        functools.partial(_ffn_kernel, tf=tf, final=final),
        grid=(n // tm,),
        in_specs=[
            row,
            pl.BlockSpec((1, N_MOD, d), lambda i: (i * tm // seq_len, 0, 0)),
            _const_spec((1, d)), _const_spec((d, ff)), _const_spec((d, ff)), _const_spec((ff, d)),
            _const_spec((1, d)),
        ],
        out_specs=row,
        out_shape=jax.ShapeDtypeStruct((n, d), F32),
        scratch_shapes=[pltpu.VMEM((tm, d), F32)],
        compiler_params=_params(("parallel",)),
        name="ffn",
    )(x2, mod, g.reshape(1, d), wg, wu, wd, final_g.reshape(1, d))


def _router_kernel(x_ref, mod_ref, g_ref, wr_ref, h_ref, r_ref, *, n_exp):
    h = _norm_mod(x_ref[...], g_ref[...], mod_ref[0, 3:4, :], mod_ref[0, 4:5, :])
    h_ref[...] = h.astype(BF16)
    logits = _dot_split(h, wr_ref[...])
    lane = lax.broadcasted_iota(I32, logits.shape, 1)
    lane_f = lane.astype(F32)
    neg = -jnp.inf
    l1 = jnp.where(lane < n_exp, logits, neg)
    m1 = l1.max(axis=-1, keepdims=True)
    i1 = jnp.where(l1 == m1, lane_f, float(LANES)).min(axis=-1, keepdims=True)
    l2 = jnp.where(lane_f == i1, neg, l1)
    m2 = l2.max(axis=-1, keepdims=True)
    i2 = jnp.where(l2 == m2, lane_f, float(LANES)).min(axis=-1, keepdims=True)
    e = jnp.exp(m2 - m1)
    w1 = 1.0 / (1.0 + e)
    w2 = e * w1
    out = jnp.where(lane == 0, i1, 0.0)
    out = jnp.where(lane == 1, i2, out)
    out = jnp.where(lane == 2, w1, out)
    out = jnp.where(lane == 3, w2, out)
    r_ref[...] = out


def _router(x2, mod, g, w_router, seq_len):
    n, d = x2.shape
    n_exp = w_router.shape[1]
    tm = min(ROW_TILE, seq_len)
    wr = jnp.zeros((d, LANES), F32).at[:, :n_exp].set(w_router)
    row = lambda w: pl.BlockSpec((tm, w), lambda i: (i, 0))
    return pl.pallas_call(
        functools.partial(_router_kernel, n_exp=n_exp),
        grid=(n // tm,),
        in_specs=[
            row(d),
            pl.BlockSpec((1, N_MOD, d), lambda i: (i * tm // seq_len, 0, 0)),
            _const_spec((1, d)), _const_spec((d, LANES)),
        ],
        out_specs=[row(d), row(LANES)],
        out_shape=[jax.ShapeDtypeStruct((n, d), BF16), jax.ShapeDtypeStruct((n, LANES), F32)],
        compiler_params=_params(("parallel",)),
        name="router",
    )(x2, mod, g.reshape(1, d), wr)


def _count_le(sorted_vals, v):
    return jnp.sum((sorted_vals[None, :] <= v[:, None]).astype(I32), axis=1)


def _moe_plan(route, n_exp):
    n = route.shape[0]
    te, tg, tc = MOE_EXPERT_TILE, MOE_SLOT_TILE, MOE_TOKEN_CHUNK
    assert n % tc == 0 and te % tg == 0
    idx = route[:, :TOP_K].astype(I32)
    wts = route[:, TOP_K:2 * TOP_K]
    ex = jnp.arange(n_exp, dtype=I32)
    hit = idx[:, :, None] == ex[None, None, :]
    sel = hit.any(axis=1)
    gate = jnp.sum(jnp.where(hit, wts[:, :, None], 0.0), axis=1)
    seli = sel.astype(I32)
    cnt = seli.sum(axis=0)
    padded = (cnt + te - 1) // te * te
    ends = jnp.cumsum(padded)
    off = ends - padded
    total = ends[-1]
    rank = jnp.cumsum(seli, axis=0) - seli
    pos = jnp.where(sel, off[None, :] + rank, -1)
    n_slots = TOP_K * n + n_exp * te
    n_tg, n_te, n_ch = n_slots // tg, n_slots // te, n // tc
    g_max = n_exp * n_ch + n_tg

    tok_of_slot = jnp.zeros((n_slots,), I32).at[jnp.where(sel, pos, n_slots)].set(
        jnp.broadcast_to(jnp.arange(n, dtype=I32)[:, None], pos.shape), mode="drop")

    te_base = jnp.arange(n_te, dtype=I32) * te
    te_valid = (te_base < total).astype(I32)
    last_valid = jnp.maximum(total // te - 1, 0)
    te_exp = jnp.minimum(_count_le(ends, jnp.minimum(te_base, last_valid * te)), n_exp - 1).astype(I32)

    tg_base = jnp.arange(n_tg, dtype=I32) * tg
    tg_exp = jnp.minimum(_count_le(ends, tg_base), n_exp - 1).astype(I32)
    used_end = off[tg_exp] + cnt[tg_exp]
    has = tg_base < used_end
    t_first = tok_of_slot[jnp.minimum(tg_base, n_slots - 1)]
    t_last = tok_of_slot[jnp.clip(jnp.minimum(tg_base + tg, used_end) - 1, 0, n_slots - 1)]
    c_lo = jnp.where(has, t_first // tc, 0)
    c_hi = jnp.where(has, t_last // tc, 0)
    n_it = c_hi - c_lo + 1
    start = jnp.cumsum(n_it) - n_it
    tot_g = start[-1] + n_it[-1]
    g = jnp.arange(g_max, dtype=I32)
    gc = jnp.minimum(g, tot_g - 1)
    s_of = (_count_le(start, gc) - 1).astype(I32)
    ga_valid = (g < tot_g).astype(I32)
    ga_tile = s_of
    ga_chunk = (c_lo[s_of] + gc - start[s_of]).astype(I32)
    ga_first = ((gc == start[s_of]) & (g < tot_g)).astype(I32)

    posc = pos.reshape(n_ch, tc, n_exp)
    hasc = (posc >= 0).any(axis=1)
    pmin = jnp.where(posc >= 0, posc, n_slots).min(axis=1)
    pmax = posc.max(axis=1)
    s_lo = jnp.where(hasc, pmin // tg, 0).reshape(-1)
    s_hi = jnp.where(hasc, pmax // tg, 0).reshape(-1)
    ccnt = jnp.where(hasc.reshape(-1), s_hi - s_lo + 1, 0)
    cend = jnp.cumsum(ccnt)
    cstart = cend - ccnt
    tot_c = cend[-1]
    hc = jnp.minimum(g, tot_c - 1)
    pair = (_count_le(cstart, hc) - 1).astype(I32)
    co_valid = (g < tot_c).astype(I32)
    co_chunk = (pair // n_exp).astype(I32)
    co_exp = (pair % n_exp).astype(I32)
    co_tile = (s_lo[pair] + hc - cstart[pair]).astype(I32)
    chunk_start = cstart.reshape(n_ch, n_exp)[:, 0]
    chunk_end = cend.reshape(n_ch, n_exp)[:, -1]
    co_first = ((hc == chunk_start[co_chunk]) & (g < tot_c)).astype(I32)
    co_last = ((hc == chunk_end[co_chunk] - 1) & (g < tot_c)).astype(I32)

    pos_t = pos.T.astype(I32)
    posw = jnp.zeros((n, LANES), F32).at[:, :n_exp].set(pos.astype(F32)).at[:, n_exp:2 * n_exp].set(gate)
    return dict(n_slots=n_slots, g_max=g_max, te_exp=te_exp, te_valid=te_valid, tg_exp=tg_exp,
                ga=(ga_tile, ga_chunk, ga_first, ga_valid), co=(co_tile, co_chunk, co_exp, co_first, co_last, co_valid),
                pos_t=pos_t, posw=posw)


def _gather_kernel(it_tile, it_chunk, it_first, it_valid, tg_exp, h_ref, pos_ref, o_ref):
    g = pl.program_id(0)
    tg, tc = MOE_SLOT_TILE, MOE_TOKEN_CHUNK

    @pl.when(it_valid[g] == 1)
    def _():
        s = it_tile[g]
        e = tg_exp[s]
        pos_row = pos_ref[pl.ds(e, 1), :]
        slot = s * tg + lax.broadcasted_iota(I32, (tg, tc), 0)
        onehot = jnp.where(pos_row == slot, 1.0, 0.0).astype(BF16)
        r = _dot(onehot, h_ref[...]).astype(BF16)

        @pl.when(it_first[g] == 1)
        def _():
            o_ref[...] = r

        @pl.when(it_first[g] == 0)
        def _():
            o_ref[...] += r


def _moe_gather(h_bf, plan):
    n, d = h_bf.shape
    n_exp = plan["pos_t"].shape[0]
    tg, tc = MOE_SLOT_TILE, MOE_TOKEN_CHUNK
    grid_spec = pltpu.PrefetchScalarGridSpec(
        num_scalar_prefetch=5,
        grid=(plan["g_max"],),
        in_specs=[
            pl.BlockSpec((tc, d), lambda g, t, c, f, v, te: (c[g], 0)),
            pl.BlockSpec((n_exp, tc), lambda g, t, c, f, v, te: (0, c[g])),
        ],
        out_specs=pl.BlockSpec((tg, d), lambda g, t, c, f, v, te: (t[g], 0)),
    )
    return pl.pallas_call(
        _gather_kernel,
        grid_spec=grid_spec,
        out_shape=jax.ShapeDtypeStruct((plan["n_slots"], d), BF16),
        compiler_params=_params(("arbitrary",)),
        name="moe_gather",
    )(*plan["ga"], plan["tg_exp"], h_bf, plan["pos_t"])


def _experts_kernel(te_exp, te_valid, h_ref, wg_ref, wu_ref, wd_ref, y_ref, acc_ref):
    i = pl.program_id(0)
    f = pl.program_id(1)

    @pl.when(f == 0)
    def _():
        acc_ref[...] = jnp.zeros_like(acc_ref)

    @pl.when(te_valid[i] == 1)
    def _():
        h = h_ref[...]
        g = _dot(h, wg_ref[0])
        u = _dot(h, wu_ref[0])
        a = (g * jax.nn.sigmoid(g) * u).astype(BF16)
        acc_ref[...] += _dot(a, wd_ref[0])

    @pl.when(f == pl.num_programs(1) - 1)
    def _():
        y_ref[...] = acc_ref[...].astype(BF16)


def _moe_experts(hs, wg, wu, wd, plan):
    n_slots, d = hs.shape
    ff = wg.shape[2]
    te = MOE_EXPERT_TILE
    tf = MOE_FF_TILE if ff % MOE_FF_TILE == 0 else LANES
    grid_spec = pltpu.PrefetchScalarGridSpec(
        num_scalar_prefetch=2,
        grid=(n_slots // te, ff // tf),
        in_specs=[
            pl.BlockSpec((te, d), lambda i, f, e, v: (i, 0)),
            pl.BlockSpec((1, d, tf), lambda i, f, e, v: (e[i], 0, f * v[i])),
            pl.BlockSpec((1, d, tf), lambda i, f, e, v: (e[i], 0, f * v[i])),
            pl.BlockSpec((1, tf, d), lambda i, f, e, v: (e[i], f * v[i], 0)),
        ],
        out_specs=pl.BlockSpec((te, d), lambda i, f, e, v: (i, 0)),
        scratch_shapes=[pltpu.VMEM((te, d), F32)],
    )
    return pl.pallas_call(
        _experts_kernel,
        grid_spec=grid_spec,
        out_shape=jax.ShapeDtypeStruct((n_slots, d), BF16),
        compiler_params=_params(("arbitrary", "arbitrary")),
        name="moe_experts",
    )(plan["te_exp"], plan["te_valid"], hs, wg, wu, wd)


def _combine_kernel(it_tile, it_chunk, it_exp, it_first, it_last, it_valid,
                    y_ref, pw_ref, x_ref, mod_ref, fg_ref, o_ref, acc_ref, *, n_exp, final):
    g = pl.program_id(0)
    tg, tc = MOE_SLOT_TILE, MOE_TOKEN_CHUNK

    @pl.when(it_first[g] == 1)
    def _():
        acc_ref[...] = jnp.zeros_like(acc_ref)

    @pl.when(it_valid[g] == 1)
    def _():
        e = it_exp[g]
        pw = pw_ref[...]
        lane = lax.broadcasted_iota(I32, pw.shape, 1)
        pos_col = jnp.sum(jnp.where(lane == e, pw, 0.0), axis=-1, keepdims=True)
        w_col = jnp.sum(jnp.where(lane == e + n_exp, pw, 0.0), axis=-1, keepdims=True)
        slot = (it_tile[g] * tg + lax.broadcasted_iota(I32, (tc, tg), 1)).astype(F32)
        onehot = jnp.where(pos_col == slot, 1.0, 0.0).astype(BF16)
        acc_ref[...] += w_col * _dot(onehot, y_ref[...])

    @pl.when(it_last[g] == 1)
    def _():
        out = x_ref[...] + mod_ref[0, 5:6, :] * acc_ref[...]
        o_ref[...] = _rmsnorm(out, fg_ref[...]) if final else out


def _moe_combine(y, x2, mod, final_g, plan, seq_len, final):
    n, d = x2.shape
    n_exp = plan["pos_t"].shape[0]
    tg, tc = MOE_SLOT_TILE, MOE_TOKEN_CHUNK
    assert seq_len % tc == 0
    grid_spec = pltpu.PrefetchScalarGridSpec(
        num_scalar_prefetch=6,
        grid=(plan["g_max"],),
        in_specs=[
            pl.BlockSpec((tg, d), lambda g, t, c, *_: (t[g], 0)),
            pl.BlockSpec((tc, LANES), lambda g, t, c, *_: (c[g], 0)),
            pl.BlockSpec((tc, d), lambda g, t, c, *_: (c[g], 0)),
            pl.BlockSpec((1, N_MOD, d), lambda g, t, c, *_: (c[g] * tc // seq_len, 0, 0)),
            pl.BlockSpec((1, d), lambda g, *_: (0, 0)),
        ],
        out_specs=pl.BlockSpec((tc, d), lambda g, t, c, *_: (c[g], 0)),
        scratch_shapes=[pltpu.VMEM((tc, d), F32)],
    )
    return pl.pallas_call(
        functools.partial(_combine_kernel, n_exp=n_exp, final=final),
        grid_spec=grid_spec,
        out_shape=jax.ShapeDtypeStruct((n, d), F32),
        compiler_params=_params(("arbitrary",)),
        name="moe_combine",
    )(*plan["co"], y, plan["posw"], x2, mod, final_g.reshape(1, d))


def _token_mix(x2, mod, l, seq_len, bt, p, kv_ctx, tabs, bias):
    a0, b0, q, k, v, zu, zb, zc, gates = _inproj(x2, mod, p["norm1_g"][l], p["w_in"][l], tabs[3], seq_len)
    fm = _fourier(a0, b0, bt, seq_len, *tabs[:3])
    if kv_ctx is None:
        attn = _ctxattn(q, k, v, bt, seq_len)
    else:
        attn = _natten(q, k, v, kv_ctx[0], kv_ctx[1], bias, bt, seq_len, kv_ctx[0].shape[0] // bt)
    out = _merge(fm, attn, zu, zb, zc, gates, x2, mod, p["conv_w"][l], p["w_fourier"][l], p["w_na"][l],
                 p["w_conv_out"][l], p["w_out"][l], seq_len)
    return out, (k, v)


def kernel(x, c, ctx, c_ctx, norm1_g, norm2_g, w_ada, b_ada, w_in, conv_w, na_rpb, w_fourier, w_na, w_conv_out, w_out, ffn_w_gate, ffn_w_up, ffn_w_down, moe_router, moe_w_gate, moe_w_up, moe_w_down, final_g):
    bt, seq, d = x.shape
    lc = ctx.shape[1]
    depth = w_in.shape[0]
    n_exp = moe_router.shape[-1]
    bf = lambda w: w.astype(BF16)
    p = dict(norm1_g=norm1_g, w_in=bf(w_in), conv_w=conv_w, w_fourier=bf(w_fourier), w_na=bf(w_na),
             w_conv_out=bf(w_conv_out), w_out=bf(w_out))
    ffn_g, ffn_u, ffn_d = bf(ffn_w_gate), bf(ffn_w_up), bf(ffn_w_down)
    moe_g, moe_u, moe_d = bf(moe_w_gate), bf(moe_w_up), bf(moe_w_down)

    rows8 = -(-(bt + 1) // 8) * 8
    cvec = jnp.zeros((rows8, d), F32).at[:bt].set(c).at[bt].set(c_ctx)
    mods = _adaln(cvec, w_ada, b_ada)

    tabs_x = _fft_tables(seq)
    tabs_c = _fft_tables(lc)
    x2 = x.reshape(bt * seq, d)
    c2 = ctx.reshape(bt * lc, d)
    for l in range(depth):
        last = l == depth - 1
        mod_x = mods[l, :bt].reshape(bt, N_MOD, d)
        mod_c = jnp.broadcast_to(mods[l, bt].reshape(1, N_MOD, d), (bt, N_MOD, d))
        bias = _bias_tables(na_rpb[l], seq // GRID_W)
        dense = l % 2 == 0
        j = l // 2
        if last:
            outs = _inproj(c2, mod_c, norm1_g[l], p["w_in"][l], tabs_c[3], lc)
            kv_ctx = (outs[3], outs[4])
        else:
            c_mid, kv_ctx = _token_mix(c2, mod_c, l, lc, bt, p, None, tabs_c, None)
            if not dense:
                raise NotImplementedError("MoE channel mixer on context tokens (only reached when depth > 2)")
            c2 = _ffn(c_mid, mod_c, norm2_g[l], ffn_g[j], ffn_u[j], ffn_d[j], final_g, lc, False)
        x_mid, _ = _token_mix(x2, mod_x, l, seq, bt, p, kv_ctx, tabs_x, bias)
        if dense:
            x2 = _ffn(x_mid, mod_x, norm2_g[l], ffn_g[j], ffn_u[j], ffn_d[j], final_g, seq, last)
        else:
            h_bf, route = _router(x_mid, mod_x, norm2_g[l], moe_router[j], seq)
            plan = _moe_plan(route, n_exp)
            hs = _moe_gather(h_bf, plan)
            y = _moe_experts(hs, moe_g[j], moe_u[j], moe_d[j], plan)
            x2 = _moe_combine(y, x_mid, mod_x, final_g, plan, seq, last)
    return x2.reshape(bt, seq, d)
```

```python
import functools

import numpy as np
import jax
import jax.numpy as jnp
from jax import lax
from jax.experimental import pallas as pl
from jax.experimental.pallas import tpu as pltpu

F32 = jnp.float32
BF16 = jnp.bfloat16
I32 = jnp.int32

EPS = 1e-6
N_MOD = 6
GRID_W = 64
WIN_H = 8
WIN_W = 16
F_GROUPS = 4
F_GROUP_DIM = 64
F_WIDTH = F_GROUPS * F_GROUP_DIM
NA_HEADS = 8
NA_HEAD_DIM = 64
NA_WIDTH = NA_HEADS * NA_HEAD_DIM
CONV_WIDTH = 256
CONV_K = 3
N_BRANCH = 3
TOP_K = 2
Q_OFF = F_WIDTH
K_OFF = Q_OFF + NA_WIDTH
V_OFF = K_OFF + NA_WIDTH
U_OFF = V_OFF + NA_WIDTH
B_OFF = U_OFF + CONV_WIDTH
C_OFF = B_OFF + CONV_WIDTH
G_OFF = C_OFF + CONV_WIDTH

LANES = 128
BF16_SUBLANES = 16
VMEM_LIMIT_BYTES = 56 * 1024 * 1024

ROW_TILE = 512
Q_ROWS = 8
KEY_ROWS = 16
MASK_VALUE = -1e30
MOE_EXPERT_TILE = 512
MOE_SLOT_TILE = 256
MOE_TOKEN_CHUNK = 512
MOE_FF_TILE = 512


def _params(sem):
    return pltpu.CompilerParams(dimension_semantics=sem, vmem_limit_bytes=VMEM_LIMIT_BYTES)


def _const_spec(shape):
    nd = len(shape)
    return pl.BlockSpec(shape, lambda *_: (0,) * nd, pipeline_mode=pl.Buffered(1))


def _dot(a, b):
    return jnp.dot(a, b, preferred_element_type=F32)


def _split_bf16(x):
    hi = x.astype(BF16)
    lo = (x - hi.astype(F32)).astype(BF16)
    return hi, lo


def _dot_split(a, w):
    ah, al = _split_bf16(a)
    wh, wl = _split_bf16(w)
    return _dot(ah, wh) + (_dot(ah, wl) + _dot(al, wh))


def _norm_mod(x, g, shift, scale):
    ms = jnp.mean(x * x, axis=-1, keepdims=True)
    y = x * lax.rsqrt(ms + EPS) * g
    return y * (1.0 + scale) + shift


def _rmsnorm(x, g):
    ms = jnp.mean(x * x, axis=-1, keepdims=True)
    return x * lax.rsqrt(ms + EPS) * g


def _adaln_kernel(c_ref, w_ref, b_ref, o_ref):
    a = c_ref[...]
    a = a * jax.nn.sigmoid(a)
    o_ref[0] = _dot_split(a, w_ref[0]) + b_ref[0]


def _adaln(cvec, w_ada, b_ada):
    depth, d, width = w_ada.shape
    rows = cvec.shape[0]
    tn = width // 4
    return pl.pallas_call(
        _adaln_kernel,
        grid=(depth, width // tn),
        in_specs=[
            pl.BlockSpec((rows, d), lambda l, j: (0, 0)),
            pl.BlockSpec((1, d, tn), lambda l, j: (l, 0, j)),
            pl.BlockSpec((1, 1, tn), lambda l, j: (l, 0, j)),
        ],
        out_specs=pl.BlockSpec((1, rows, tn), lambda l, j: (l, 0, j)),
        out_shape=jax.ShapeDtypeStruct((depth, rows, width), F32),
        compiler_params=_params(("arbitrary", "arbitrary")),
        name="adaln",
    )(cvec, w_ada, b_ada.reshape(depth, 1, width))


def _inproj_kernel(x_ref, mod_ref, g_ref, w_ref, bd_ref,
                   a0_ref, b0_ref, q_ref, k_ref, v_ref, u_ref, b_ref, c_ref, gate_ref, *, d):
    h = _norm_mod(x_ref[...], g_ref[...], mod_ref[0, 0:1, :], mod_ref[0, 1:2, :]).astype(BF16)

    def proj(c0, c1):
        return _dot(h, w_ref[:, c0:c1])

    zf = proj(0, Q_OFF).astype(BF16)
    ab = _dot(zf, bd_ref[...])
    a0_ref[...] = ab[:, :F_WIDTH].astype(BF16)
    b0_ref[...] = ab[:, F_WIDTH:].astype(BF16)
    q_ref[...] = (proj(Q_OFF, K_OFF) * (NA_HEAD_DIM ** -0.5)).astype(BF16)
    k_ref[...] = proj(K_OFF, V_OFF).astype(BF16)
    v_ref[...] = proj(V_OFF, U_OFF).astype(BF16)
    u_ref[...] = proj(U_OFF, B_OFF).astype(BF16)
    b_ref[...] = proj(B_OFF, C_OFF).astype(BF16)
    c_ref[...] = proj(C_OFF, G_OFF).astype(BF16)
    step = 512 if d % 512 == 0 else d
    for j in range(0, N_BRANCH * d, step):
        gate_ref[:, j:j + step] = jax.nn.sigmoid(proj(G_OFF + j, G_OFF + j + step)).astype(BF16)


def _inproj(x2, mod, g, w_bf, bd, seq_len):
    n, d = x2.shape
    tm = min(ROW_TILE, seq_len)
    pw = w_bf.shape[1]
    widths = [F_WIDTH, F_WIDTH, NA_WIDTH, NA_WIDTH, NA_WIDTH, CONV_WIDTH, CONV_WIDTH, CONV_WIDTH, N_BRANCH * d]
    row = lambda w: pl.BlockSpec((tm, w), lambda i: (i, 0))
    return pl.pallas_call(
        functools.partial(_inproj_kernel, d=d),
        grid=(n // tm,),
        in_specs=[
            row(d),
            pl.BlockSpec((1, N_MOD, d), lambda i: (i * tm // seq_len, 0, 0)),
            _const_spec((1, d)),
            _const_spec((d, pw)),
            _const_spec((F_WIDTH, 2 * F_WIDTH)),
        ],
        out_specs=[row(w) for w in widths],
        out_shape=[jax.ShapeDtypeStruct((n, w), BF16) for w in widths],
        compiler_params=_params(("parallel",)),
        name="inproj",
    )(x2, mod, g.reshape(1, d), w_bf, bd)


def _fft_split(seq_len):
    b = 64 if seq_len >= 1024 else 16
    return seq_len // b, b


def _fft_tables(seq_len):
    a_pts, b_pts = _fft_split(seq_len)
    ia = np.arange(a_pts)
    ang = 2.0 * np.pi * ((ia[:, None] * ia[None, :]) % a_pts) / a_pts
    c, s = np.cos(ang) / np.sqrt(a_pts), np.sin(ang) / np.sqrt(a_pts)
    w1 = np.block([[c, -s], [-s, -c]])
    ib = np.arange(b_pts)
    num = (ib[None, :, None] * ib[None, None, :] * a_pts + ib[None, None, :] * ia[:, None, None]) % seq_len
    th = 2.0 * np.pi * num / seq_len
    mc, ms = np.cos(th) / np.sqrt(b_pts), np.sin(th) / np.sqrt(b_pts)
    ic = np.arange(F_GROUP_DIM)
    angc = 2.0 * np.pi * ((ic[:, None] * ic[None, :]) % F_GROUP_DIM) / F_GROUP_DIM
    eye = np.eye(F_GROUPS)
    bd = np.concatenate([np.kron(eye, np.cos(angc)), np.kron(eye, np.sin(angc))], axis=1) / np.sqrt(F_GROUP_DIM)
    return tuple(jnp.asarray(t, F32).astype(BF16) for t in (w1, mc, ms, bd))


def _fft1_kernel(a_ref, b_ref, w_ref, o_ref, *, a_pts):
    x = jnp.concatenate([a_ref[0], b_ref[0]], axis=0)
    t = _dot(w_ref[...], x)
    o_ref[0, 0] = t[:a_pts].astype(BF16)
    o_ref[0, 1] = t[a_pts:].astype(BF16)


def _fft2_kernel(tr_ref, ti_ref, mc_ref, ms_ref, o_ref, *, ac):
    for j in range(ac):
        y = _dot(mc_ref[j], tr_ref[0, 0, j]) + _dot(ms_ref[j], ti_ref[0, 0, j])
        o_ref[0, j] = y.astype(BF16)


def _fourier(a0, b0, bt, seq_len, w1, mc, ms):
    a_pts, b_pts = _fft_split(seq_len)
    c = F_WIDTH
    bc = b_pts * c
    cw = min(2048, bc)
    view = lambda z: z.reshape(bt, a_pts, bc)
    t = pl.pallas_call(
        functools.partial(_fft1_kernel, a_pts=a_pts),
        grid=(bt, bc // cw),
        in_specs=[
            pl.BlockSpec((1, a_pts, cw), lambda b, j: (b, 0, j)),
            pl.BlockSpec((1, a_pts, cw), lambda b, j: (b, 0, j)),
            _const_spec((2 * a_pts, 2 * a_pts)),
        ],
        out_specs=pl.BlockSpec((1, 2, a_pts, cw), lambda b, j: (b, 0, 0, j)),
        out_shape=jax.ShapeDtypeStruct((bt, 2, a_pts, bc), BF16),
        compiler_params=_params(("parallel", "parallel")),
        name="fft1",
    )(view(a0), view(b0), w1)
    t5 = t.reshape(bt, 2, a_pts, b_pts, c)
    ac = min(a_pts, 16)
    out = pl.pallas_call(
        functools.partial(_fft2_kernel, ac=ac),
        grid=(a_pts // ac, bt),
        in_specs=[
            pl.BlockSpec((1, 1, ac, b_pts, c), lambda i, b: (b, 0, i, 0, 0)),
            pl.BlockSpec((1, 1, ac, b_pts, c), lambda i, b: (b, 1, i, 0, 0)),
            pl.BlockSpec((ac, b_pts, b_pts), lambda i, b: (i, 0, 0)),
            pl.BlockSpec((ac, b_pts, b_pts), lambda i, b: (i, 0, 0)),
        ],
        out_specs=pl.BlockSpec((1, ac, b_pts, c), lambda i, b: (b, i, 0, 0)),
        out_shape=jax.ShapeDtypeStruct((bt, a_pts, b_pts, c), BF16),
        compiler_params=_params(("parallel", "parallel")),
        name="fft2",
    )(t5, t5, mc, ms)
    return out.reshape(bt * a_pts, bc)


def _window_rows(rows):
    r0s = np.array([0, Q_ROWS, rows - Q_ROWS])
    i = np.arange(Q_ROWS)
    j = np.arange(KEY_ROWS)
    r = r0s[:, None, None] + i[None, :, None]
    kr = r0s[:, None, None] - WIN_H // 2 + j[None, None, :]
    base = np.clip(r - WIN_H // 2, 0, rows - WIN_H)
    ok = (kr >= 0) & (kr < rows) & (kr >= base) & (kr < base + WIN_H)
    return tuple(tuple(tuple(bool(b) for b in row) for row in blk) for blk in ok)


def _biastab_kernel(rpb_ref, o_ref, *, ok_r):
    w = GRID_W
    cq = lax.broadcasted_iota(I32, (w, LANES), 0)
    kc = lax.broadcasted_iota(I32, (w, LANES), 1)
    cs = jnp.clip(cq - WIN_W // 2, 0, w - WIN_W)
    ok_c = (kc >= cs) & (kc < cs + WIN_W)
    low = kc < w
    masked = jnp.full((w, LANES), MASK_VALUE, F32)
    lo, hi = [], []
    for dr in range(2 * WIN_H - 1):
        xb = jnp.broadcast_to(rpb_ref[0, dr:dr + 1, :], (w, LANES))
        t = pltpu.roll(xb, LANES - (WIN_W - 1), 1, stride=1, stride_axis=0)
        t = jnp.where(ok_c, t, MASK_VALUE)
        lo.append(t)
        hi.append(pltpu.roll(t, w, 1))
    for v in range(3):
        for i in range(Q_ROWS):
            for jj in range(KEY_ROWS // 2):
                j0 = 2 * jj
                d0 = j0 - i + WIN_H // 2 - 1
                a = lo[d0] if ok_r[v][i][j0] else masked
                b = hi[d0 + 1] if ok_r[v][i][j0 + 1] else masked
                blk = masked if (a is masked and b is masked) else jnp.where(low, a, b)
                o_ref[v, 0, i * w:(i + 1) * w, jj * LANES:(jj + 1) * LANES] = blk


def _bias_tables(rpb, rows):
    n_dr, n_dc = rpb.shape[1], rpb.shape[2]
    rp = jnp.zeros((NA_HEADS, 2 * WIN_H, LANES), F32).at[:, :n_dr, :n_dc].set(rpb.astype(F32))
    qb, wk = Q_ROWS * GRID_W, KEY_ROWS * GRID_W
    return pl.pallas_call(
        functools.partial(_biastab_kernel, ok_r=_window_rows(rows)),
        grid=(NA_HEADS,),
        in_specs=[pl.BlockSpec((1, 2 * WIN_H, LANES), lambda h: (h, 0, 0))],
        out_specs=pl.BlockSpec((3, 1, qb, wk), lambda h: (0, h, 0, 0)),
        out_shape=jax.ShapeDtypeStruct((3, NA_HEADS, qb, wk), F32),
        compiler_params=_params(("parallel",)),
        name="biastab",
    )(rp)


def _softmax_pv(s_parts, v_parts):
    m = s_parts[0].max(axis=-1, keepdims=True)
    for s in s_parts[1:]:
        m = jnp.maximum(m, s.max(axis=-1, keepdims=True))
    l = None
    pv = None
    for s, v in zip(s_parts, v_parts):
        p = jnp.exp(s - m)
        ls = p.sum(axis=-1, keepdims=True)
        o = _dot(p.astype(BF16), v)
        l = ls if l is None else l + ls
        pv = o if pv is None else pv + o
    return pv * (1.0 / l)


def _natten_kernel(q_ref, kp_ref, kc_ref, kn_ref, vp_ref, vc_ref, vn_ref, kx_ref, vx_ref, bias_ref,
                   o_ref, kw_ref, vw_ref, *, lc):
    qb = Q_ROWS * GRID_W
    halo = (KEY_ROWS - Q_ROWS) // 2 * GRID_W
    wk = KEY_ROWS * GRID_W
    for dst, (p, c, n, x) in ((kw_ref, (kp_ref, kc_ref, kn_ref, kx_ref)), (vw_ref, (vp_ref, vc_ref, vn_ref, vx_ref))):
        dst[0:halo] = p[qb - halo:qb]
        dst[halo:halo + qb] = c[...]
        dst[halo + qb:wk] = n[0:halo]
        dst[wk:wk + lc] = x[...]
    lane = lax.broadcasted_iota(I32, (qb, LANES), 1)
    low = lane < NA_HEAD_DIM
    for hp in range(NA_HEADS // 2):
        sl = slice(LANES * hp, LANES * (hp + 1))
        qp = q_ref[:, sl]
        kw = kw_ref[:, sl]
        vw = vw_ref[:, sl]
        outs = []
        for sub in range(2):
            qm = jnp.where(low if sub == 0 else jnp.logical_not(low), qp, jnp.zeros_like(qp))
            s = lax.dot_general(qm, kw, (((1,), (1,)), ((), ())), preferred_element_type=F32)
            s_win = s[:, :wk] + bias_ref[0, 2 * hp + sub]
            outs.append(_softmax_pv([s_win, s[:, wk:]], [vw[:wk], vw[wk:]]))
        o_ref[:, sl] = jnp.where(low, outs[0], outs[1]).astype(BF16)


def _natten(q, k, v, kx, vx, bias, bt, seq_len, lc):
    n = q.shape[0]
    qb = Q_ROWS * GRID_W
    nb = seq_len // qb
    assert seq_len % qb == 0 and nb >= 2, "neighbourhood attention kernel needs >= 16 grid rows, a multiple of 8"
    wk = KEY_ROWS * GRID_W
    cur = lambda b, r: (b * nb + r, 0)
    prev = lambda b, r: (b * nb + jnp.maximum(r - 1, 0), 0)
    nxt = lambda b, r: (b * nb + jnp.minimum(r + 1, nb - 1), 0)
    blk = lambda im: pl.BlockSpec((qb, NA_WIDTH), im)
    variant = lambda b, r: (jnp.where(r == 0, 0, jnp.where(r == nb - 1, 2, 1)), 0, 0, 0)
    return pl.pallas_call(
        functools.partial(_natten_kernel, lc=lc),
        grid=(bt, nb),
        in_specs=[
            blk(cur), blk(prev), blk(cur), blk(nxt), blk(prev), blk(cur), blk(nxt),
            pl.BlockSpec((lc, NA_WIDTH), lambda b, r: (b, 0)),
            pl.BlockSpec((lc, NA_WIDTH), lambda b, r: (b, 0)),
            pl.BlockSpec((1, NA_HEADS, qb, wk), variant, pipeline_mode=pl.Buffered(1)),
        ],
        out_specs=blk(cur),
        out_shape=jax.ShapeDtypeStruct((n, NA_WIDTH), BF16),
        scratch_shapes=[pltpu.VMEM((wk + lc, NA_WIDTH), BF16), pltpu.VMEM((wk + lc, NA_WIDTH), BF16)],
        compiler_params=_params(("parallel", "arbitrary")),
        name="natten",
    )(q, k, k, k, v, v, v, kx, vx, bias)


def _ctxattn_kernel(q_ref, k_ref, v_ref, o_ref):
    lc = q_ref.shape[0]
    lane = lax.broadcasted_iota(I32, (lc, LANES), 1)
    low = lane < NA_HEAD_DIM
    for hp in range(NA_HEADS // 2):
        sl = slice(LANES * hp, LANES * (hp + 1))
        qp = q_ref[:, sl]
        kw = k_ref[:, sl]
        vw = v_ref[:, sl]
        outs = []
        for sub in range(2):
            qm = jnp.where(low if sub == 0 else jnp.logical_not(low), qp, jnp.zeros_like(qp))
            s = lax.dot_general(qm, kw, (((1,), (1,)), ((), ())), preferred_element_type=F32)
            outs.append(_softmax_pv([s], [vw]))
        o_ref[:, sl] = jnp.where(low, outs[0], outs[1]).astype(BF16)


def _ctxattn(q, k, v, bt, lc):
    blk = pl.BlockSpec((lc, NA_WIDTH), lambda b: (b, 0))
    return pl.pallas_call(
        _ctxattn_kernel,
        grid=(bt,),
        in_specs=[blk, blk, blk],
        out_specs=blk,
        out_shape=jax.ShapeDtypeStruct(q.shape, BF16),
        compiler_params=_params(("parallel",)),
        name="ctxattn",
    )(q, k, v)


def _merge_kernel(fm_ref, at_ref, u_ref, b_ref, c_ref, up_ref, cp_ref, un_ref, cn_ref, gate_ref, x_ref, mod_ref,
                  cw_ref, wf_ref, wa_ref, wc_ref, wo_ref, o_ref, pre_ref, *, tm, d, a_pts, seq_len):
    i = pl.program_id(0)
    t0 = i * tm
    first = (t0 % seq_len) == 0
    last = ((t0 + tm) % seq_len) == 0
    p = c_ref[...].astype(F32) * u_ref[...].astype(F32)
    hs = BF16_SUBLANES
    p_above = cp_ref[hs - 1:hs, :].astype(F32) * up_ref[hs - 1:hs, :].astype(F32)
    p_below = cn_ref[0:1, :].astype(F32) * un_ref[0:1, :].astype(F32)
    p_above = jnp.where(first, 0.0, p_above)
    p_below = jnp.where(last, 0.0, p_below)
    row = lax.broadcasted_iota(I32, (tm, CONV_WIDTH), 0)
    p_dn = jnp.where(row == 0, p_above, pltpu.roll(p, 1, 0))
    p_up = jnp.where(row == tm - 1, p_below, pltpu.roll(p, tm - 1, 0))
    cv = cw_ref[0:1, :] * p_dn + cw_ref[1:2, :] * p + cw_ref[2:3, :] * p_up
    y_c = _dot((b_ref[...].astype(F32) * cv).astype(BF16), wc_ref[...])
    y_a = _dot(at_ref[...], wa_ref[...])
    pre_ref[...] = (gate_ref[:, d:2 * d].astype(F32) * y_a + gate_ref[:, 2 * d:3 * d].astype(F32) * y_c)
    for j in range(tm // a_pts):
        rs = slice(j * a_pts, (j + 1) * a_pts)
        y_f = _dot(fm_ref[:, j * F_WIDTH:(j + 1) * F_WIDTH], wf_ref[...])
        pre_ref[rs, :] += gate_ref[rs, 0:d].astype(F32) * y_f
    mix = _dot(pre_ref[...].astype(BF16), wo_ref[...])
    o_ref[...] = x_ref[...] + mod_ref[0, 2:3, :] * mix


def _merge(fm, attn, zu, zb, zc, gates, x2, mod, conv_w, wf, wa, wc, wo, seq_len):
    n, d = x2.shape
    tm = min(ROW_TILE, seq_len)
    a_pts, b_pts = _fft_split(seq_len)
    assert tm % a_pts == 0
    hs = BF16_SUBLANES
    nh = n // hs
    per = tm // hs
    row = lambda w: pl.BlockSpec((tm, w), lambda i: (i, 0))
    above = pl.BlockSpec((hs, CONV_WIDTH), lambda i: (jnp.maximum(i * per - 1, 0), 0))
    below = pl.BlockSpec((hs, CONV_WIDTH), lambda i: (jnp.minimum((i + 1) * per, nh - 1), 0))
    tiles_per_seq = seq_len // tm
    sub = tm // a_pts
    fm_spec = pl.BlockSpec((a_pts, sub * F_WIDTH), lambda i: (i // tiles_per_seq, i % tiles_per_seq))
    return pl.pallas_call(
        functools.partial(_merge_kernel, tm=tm, d=d, a_pts=a_pts, seq_len=seq_len),
        grid=(n // tm,),
        in_specs=[
            fm_spec, row(NA_WIDTH), row(CONV_WIDTH), row(CONV_WIDTH), row(CONV_WIDTH),
            above, above, below, below,
            row(N_BRANCH * d), row(d),
            pl.BlockSpec((1, N_MOD, d), lambda i: (i * tm // seq_len, 0, 0)),
            _const_spec((CONV_K, CONV_WIDTH)),
            _const_spec((F_WIDTH, d)), _const_spec((NA_WIDTH, d)), _const_spec((CONV_WIDTH, d)), _const_spec((d, d)),
        ],
        out_specs=row(d),
        out_shape=jax.ShapeDtypeStruct((n, d), F32),
        scratch_shapes=[pltpu.VMEM((tm, d), F32)],
        compiler_params=_params(("parallel",)),
        name="merge",
    )(fm, attn, zu, zb, zc, zu, zc, zu, zc, gates, x2, mod, conv_w, wf, wa, wc, wo)


def _ffn_kernel(x_ref, mod_ref, g_ref, wg_ref, wu_ref, wd_ref, fg_ref, o_ref, acc_ref, *, tf, final):
    x = x_ref[...]
    h = _norm_mod(x, g_ref[...], mod_ref[0, 3:4, :], mod_ref[0, 4:5, :]).astype(BF16)
    ff = wg_ref.shape[1]
    for f in range(0, ff, tf):
        g = _dot(h, wg_ref[:, f:f + tf])
        u = _dot(h, wu_ref[:, f:f + tf])
        a = (g * jax.nn.sigmoid(g) * u).astype(BF16)
        y = _dot(a, wd_ref[f:f + tf, :])
        if f == 0:
            acc_ref[...] = y
        else:
            acc_ref[...] += y
    out = x + mod_ref[0, 5:6, :] * acc_ref[...]
    o_ref[...] = _rmsnorm(out, fg_ref[...]) if final else out


def _ffn(x2, mod, g, wg, wu, wd, final_g, seq_len, final):
    n, d = x2.shape
    ff = wg.shape[1]
    tm = min(ROW_TILE, seq_len)
    tf = 256 if ff % 256 == 0 else LANES
    row = pl.BlockSpec((tm, d), lambda i: (i, 0))
    return pl.pallas_call(
        functools.partial(_ffn_kernel, tf=tf, final=final),
        grid=(n // tm,),
        in_specs=[
            row,
            pl.BlockSpec((1, N_MOD, d), lambda i: (i * tm // seq_len, 0, 0)),
            _const_spec((1, d)), _const_spec((d, ff)), _const_spec((d, ff)), _const_spec((ff, d)),
            _const_spec((1, d)),
        ],
        out_specs=row,
        out_shape=jax.ShapeDtypeStruct((n, d), F32),
        scratch_shapes=[pltpu.VMEM((tm, d), F32)],
        compiler_params=_params(("parallel",)),
        name="ffn",
    )(x2, mod, g.reshape(1, d), wg, wu, wd, final_g.reshape(1, d))


def _router_kernel(x_ref, mod_ref, g_ref, wr_ref, h_ref, r_ref, *, n_exp):
    h = _norm_mod(x_ref[...], g_ref[...], mod_ref[0, 3:4, :], mod_ref[0, 4:5, :])
    h_ref[...] = h.astype(BF16)
    logits = _dot_split(h, wr_ref[...])
    lane = lax.broadcasted_iota(I32, logits.shape, 1)
    lane_f = lane.astype(F32)
    neg = -jnp.inf
    l1 = jnp.where(lane < n_exp, logits, neg)
    m1 = l1.max(axis=-1, keepdims=True)
    i1 = jnp.where(l1 == m1, lane_f, float(LANES)).min(axis=-1, keepdims=True)
    l2 = jnp.where(lane_f == i1, neg, l1)
    m2 = l2.max(axis=-1, keepdims=True)
    i2 = jnp.where(l2 == m2, lane_f, float(LANES)).min(axis=-1, keepdims=True)
    e = jnp.exp(m2 - m1)
    w1 = 1.0 / (1.0 + e)
    w2 = e * w1
    out = jnp.where(lane == 0, i1, 0.0)
    out = jnp.where(lane == 1, i2, out)
    out = jnp.where(lane == 2, w1, out)
    out = jnp.where(lane == 3, w2, out)
    r_ref[...] = out


def _router(x2, mod, g, w_router, seq_len):
    n, d = x2.shape
    n_exp = w_router.shape[1]
    tm = min(ROW_TILE, seq_len)
    wr = jnp.zeros((d, LANES), F32).at[:, :n_exp].set(w_router)
    row = lambda w: pl.BlockSpec((tm, w), lambda i: (i, 0))
    return pl.pallas_call(
        functools.partial(_router_kernel, n_exp=n_exp),
        grid=(n // tm,),
        in_specs=[
            row(d),
            pl.BlockSpec((1, N_MOD, d), lambda i: (i * tm // seq_len, 0, 0)),
            _const_spec((1, d)), _const_spec((d, LANES)),
        ],
        out_specs=[row(d), row(LANES)],
        out_shape=[jax.ShapeDtypeStruct((n, d), BF16), jax.ShapeDtypeStruct((n, LANES), F32)],
        compiler_params=_params(("parallel",)),
        name="router",
    )(x2, mod, g.reshape(1, d), wr)


def _count_le(sorted_vals, v):
    return jnp.sum((sorted_vals[None, :] <= v[:, None]).astype(I32), axis=1)


def _moe_plan(route, n_exp):
    n = route.shape[0]
    te, tg, tc = MOE_EXPERT_TILE, MOE_SLOT_TILE, MOE_TOKEN_CHUNK
    assert n % tc == 0 and te % tg == 0
    idx = route[:, :TOP_K].astype(I32)
    wts = route[:, TOP_K:2 * TOP_K]
    ex = jnp.arange(n_exp, dtype=I32)
    hit = idx[:, :, None] == ex[None, None, :]
    sel = hit.any(axis=1)
    gate = jnp.sum(jnp.where(hit, wts[:, :, None], 0.0), axis=1)
    seli = sel.astype(I32)
    cnt = seli.sum(axis=0)
    padded = (cnt + te - 1) // te * te
    ends = jnp.cumsum(padded)
    off = ends - padded
    total = ends[-1]
    rank = jnp.cumsum(seli, axis=0) - seli
    pos = jnp.where(sel, off[None, :] + rank, -1)
    n_slots = TOP_K * n + n_exp * te
    n_tg, n_te, n_ch = n_slots // tg, n_slots // te, n // tc
    g_max = n_exp * n_ch + n_tg

    posc = pos.reshape(n_ch, tc, n_exp)
    hasc = (posc >= 0).any(axis=1)
    pmin = jnp.where(posc >= 0, posc, n_slots).min(axis=1)
    pmax = posc.max(axis=1)

    te_base = jnp.arange(n_te, dtype=I32) * te
    te_valid = (te_base < total).astype(I32)
    last_valid = jnp.maximum(total // te - 1, 0)
    te_exp = jnp.minimum(_count_le(ends, jnp.minimum(te_base, last_valid * te)), n_exp - 1).astype(I32)

    tg_base = jnp.arange(n_tg, dtype=I32) * tg
    tg_exp = jnp.minimum(_count_le(ends, tg_base), n_exp - 1).astype(I32)
    used_end = off[tg_exp] + cnt[tg_exp]
    has = tg_base < used_end
    ch = jnp.arange(n_ch, dtype=I32)[None, :]
    has_e = jnp.take(hasc.T, tg_exp, axis=0)
    first_c = jnp.where(has_e & (jnp.take(pmax.T, tg_exp, axis=0) >= tg_base[:, None]), ch, n_ch).min(axis=1)
    last_c = jnp.where(has_e & (jnp.take(pmin.T, tg_exp, axis=0) < tg_base[:, None] + tg), ch, -1).max(axis=1)
    c_lo = jnp.where(has, first_c, 0)
    c_hi = jnp.where(has, last_c, 0)
    n_it = c_hi - c_lo + 1
    start = jnp.cumsum(n_it) - n_it
    tot_g = start[-1] + n_it[-1]
    g = jnp.arange(g_max, dtype=I32)
    gc = jnp.minimum(g, tot_g - 1)
    s_of = (_count_le(start, gc) - 1).astype(I32)
    ga_valid = (g < tot_g).astype(I32)
    ga_tile = s_of
    ga_chunk = (c_lo[s_of] + gc - start[s_of]).astype(I32)
    ga_first = ((gc == start[s_of]) & (g < tot_g)).astype(I32)

    s_lo = jnp.where(hasc, pmin // tg, 0).reshape(-1)
    s_hi = jnp.where(hasc, pmax // tg, 0).reshape(-1)
    ccnt = jnp.where(hasc.reshape(-1), s_hi - s_lo + 1, 0)
    cend = jnp.cumsum(ccnt)
    cstart = cend - ccnt
    tot_c = cend[-1]
    hc = jnp.minimum(g, tot_c - 1)
    pair = (_count_le(cstart, hc) - 1).astype(I32)
    co_valid = (g < tot_c).astype(I32)
    co_chunk = (pair // n_exp).astype(I32)
    co_exp = (pair % n_exp).astype(I32)
    co_tile = (s_lo[pair] + hc - cstart[pair]).astype(I32)
    chunk_start = cstart.reshape(n_ch, n_exp)[:, 0]
    chunk_end = cend.reshape(n_ch, n_exp)[:, -1]
    co_first = ((hc == chunk_start[co_chunk]) & (g < tot_c)).astype(I32)
    co_last = ((hc == chunk_end[co_chunk] - 1) & (g < tot_c)).astype(I32)

    pos_t = pos.T.astype(I32)
    posw = jnp.zeros((n, LANES), F32).at[:, :n_exp].set(pos.astype(F32)).at[:, n_exp:2 * n_exp].set(gate)
    return dict(n_slots=n_slots, g_max=g_max, te_exp=te_exp, te_valid=te_valid, tg_exp=tg_exp,
                ga=(ga_tile, ga_chunk, ga_first, ga_valid), co=(co_tile, co_chunk, co_exp, co_first, co_last, co_valid),
                pos_t=pos_t, posw=posw)


def _gather_kernel(it_tile, it_chunk, it_first, it_valid, tg_exp, h_ref, pos_ref, o_ref):
    g = pl.program_id(0)
    tg, tc = MOE_SLOT_TILE, MOE_TOKEN_CHUNK

    @pl.when(it_valid[g] == 1)
    def _():
        s = it_tile[g]
        e = tg_exp[s]
        pos_row = pos_ref[pl.ds(e, 1), :]
        slot = s * tg + lax.broadcasted_iota(I32, (tg, tc), 0)
        onehot = jnp.where(pos_row == slot, 1.0, 0.0).astype(BF16)
        r = _dot(onehot, h_ref[...]).astype(BF16)

        @pl.when(it_first[g] == 1)
        def _():
            o_ref[...] = r

        @pl.when(it_first[g] == 0)
        def _():
            o_ref[...] += r


def _moe_gather(h_bf, plan):
    n, d = h_bf.shape
    n_exp = plan["pos_t"].shape[0]
    tg, tc = MOE_SLOT_TILE, MOE_TOKEN_CHUNK
    grid_spec = pltpu.PrefetchScalarGridSpec(
        num_scalar_prefetch=5,
        grid=(plan["g_max"],),
        in_specs=[
            pl.BlockSpec((tc, d), lambda g, t, c, f, v, te: (c[g], 0)),
            pl.BlockSpec((n_exp, tc), lambda g, t, c, f, v, te: (0, c[g])),
        ],
        out_specs=pl.BlockSpec((tg, d), lambda g, t, c, f, v, te: (t[g], 0)),
    )
    return pl.pallas_call(
        _gather_kernel,
        grid_spec=grid_spec,
        out_shape=jax.ShapeDtypeStruct((plan["n_slots"], d), BF16),
        compiler_params=_params(("arbitrary",)),
        name="moe_gather",
    )(*plan["ga"], plan["tg_exp"], h_bf, plan["pos_t"])


def _experts_kernel(te_exp, te_valid, h_ref, wg_ref, wu_ref, wd_ref, y_ref, acc_ref):
    i = pl.program_id(0)
    f = pl.program_id(1)

    @pl.when(f == 0)
    def _():
        acc_ref[...] = jnp.zeros_like(acc_ref)

    @pl.when(te_valid[i] == 1)
    def _():
        h = h_ref[...]
        g = _dot(h, wg_ref[0])
        u = _dot(h, wu_ref[0])
        a = (g * jax.nn.sigmoid(g) * u).astype(BF16)
        acc_ref[...] += _dot(a, wd_ref[0])

    @pl.when(f == pl.num_programs(1) - 1)
    def _():
        y_ref[...] = acc_ref[...].astype(BF16)


def _moe_experts(hs, wg, wu, wd, plan):
    n_slots, d = hs.shape
    ff = wg.shape[2]
    te = MOE_EXPERT_TILE
    tf = MOE_FF_TILE if ff % MOE_FF_TILE == 0 else LANES
    grid_spec = pltpu.PrefetchScalarGridSpec(
        num_scalar_prefetch=2,
        grid=(n_slots // te, ff // tf),
        in_specs=[
            pl.BlockSpec((te, d), lambda i, f, e, v: (i, 0)),
            pl.BlockSpec((1, d, tf), lambda i, f, e, v: (e[i], 0, f * v[i])),
            pl.BlockSpec((1, d, tf), lambda i, f, e, v: (e[i], 0, f * v[i])),
            pl.BlockSpec((1, tf, d), lambda i, f, e, v: (e[i], f * v[i], 0)),
        ],
        out_specs=pl.BlockSpec((te, d), lambda i, f, e, v: (i, 0)),
        scratch_shapes=[pltpu.VMEM((te, d), F32)],
    )
    return pl.pallas_call(
        _experts_kernel,
        grid_spec=grid_spec,
        out_shape=jax.ShapeDtypeStruct((n_slots, d), BF16),
        compiler_params=_params(("arbitrary", "arbitrary")),
        name="moe_experts",
    )(plan["te_exp"], plan["te_valid"], hs, wg, wu, wd)


def _combine_kernel(it_tile, it_chunk, it_exp, it_first, it_last, it_valid,
                    y_ref, pw_ref, x_ref, mod_ref, fg_ref, o_ref, acc_ref, *, n_exp, final):
    g = pl.program_id(0)
    tg, tc = MOE_SLOT_TILE, MOE_TOKEN_CHUNK

    @pl.when(it_first[g] == 1)
    def _():
        acc_ref[...] = jnp.zeros_like(acc_ref)

    @pl.when(it_valid[g] == 1)
    def _():
        e = it_exp[g]
        pw = pw_ref[...]
        lane = lax.broadcasted_iota(I32, pw.shape, 1)
        pos_col = jnp.sum(jnp.where(lane == e, pw, 0.0), axis=-1, keepdims=True)
        w_col = jnp.sum(jnp.where(lane == e + n_exp, pw, 0.0), axis=-1, keepdims=True)
        slot = (it_tile[g] * tg + lax.broadcasted_iota(I32, (tc, tg), 1)).astype(F32)
        onehot = jnp.where(pos_col == slot, 1.0, 0.0).astype(BF16)
        acc_ref[...] += w_col * _dot(onehot, y_ref[...])

    @pl.when(it_last[g] == 1)
    def _():
        out = x_ref[...] + mod_ref[0, 5:6, :] * acc_ref[...]
        o_ref[...] = _rmsnorm(out, fg_ref[...]) if final else out


def _moe_combine(y, x2, mod, final_g, plan, seq_len, final):
    n, d = x2.shape
    n_exp = plan["pos_t"].shape[0]
    tg, tc = MOE_SLOT_TILE, MOE_TOKEN_CHUNK
    assert seq_len % tc == 0
    grid_spec = pltpu.PrefetchScalarGridSpec(
        num_scalar_prefetch=6,
        grid=(plan["g_max"],),
        in_specs=[
            pl.BlockSpec((tg, d), lambda g, t, c, *_: (t[g], 0)),
            pl.BlockSpec((tc, LANES), lambda g, t, c, *_: (c[g], 0)),
            pl.BlockSpec((tc, d), lambda g, t, c, *_: (c[g], 0)),
            pl.BlockSpec((1, N_MOD, d), lambda g, t, c, *_: (c[g] * tc // seq_len, 0, 0)),
            pl.BlockSpec((1, d), lambda g, *_: (0, 0)),
        ],
        out_specs=pl.BlockSpec((tc, d), lambda g, t, c, *_: (c[g], 0)),
        scratch_shapes=[pltpu.VMEM((tc, d), F32)],
    )
    return pl.pallas_call(
        functools.partial(_combine_kernel, n_exp=n_exp, final=final),
        grid_spec=grid_spec,
        out_shape=jax.ShapeDtypeStruct((n, d), F32),
        compiler_params=_params(("arbitrary",)),
        name="moe_combine",
    )(*plan["co"], y, plan["posw"], x2, mod, final_g.reshape(1, d))


def _token_mix(x2, mod, l, seq_len, bt, p, kv_ctx, tabs, bias):
    a0, b0, q, k, v, zu, zb, zc, gates = _inproj(x2, mod, p["norm1_g"][l], p["w_in"][l], tabs[3], seq_len)
    fm = _fourier(a0, b0, bt, seq_len, *tabs[:3])
    if kv_ctx is None:
        attn = _ctxattn(q, k, v, bt, seq_len)
    else:
        attn = _natten(q, k, v, kv_ctx[0], kv_ctx[1], bias, bt, seq_len, kv_ctx[0].shape[0] // bt)
    out = _merge(fm, attn, zu, zb, zc, gates, x2, mod, p["conv_w"][l], p["w_fourier"][l], p["w_na"][l],
                 p["w_conv_out"][l], p["w_out"][l], seq_len)
    return out, (k, v)


def kernel(x, c, ctx, c_ctx, norm1_g, norm2_g, w_ada, b_ada, w_in, conv_w, na_rpb, w_fourier, w_na, w_conv_out, w_out, ffn_w_gate, ffn_w_up, ffn_w_down, moe_router, moe_w_gate, moe_w_up, moe_w_down, final_g):
    bt, seq, d = x.shape
    lc = ctx.shape[1]
    depth = w_in.shape[0]
    n_exp = moe_router.shape[-1]
    bf = lambda w: w.astype(BF16)
    p = dict(norm1_g=norm1_g, w_in=bf(w_in), conv_w=conv_w, w_fourier=bf(w_fourier), w_na=bf(w_na),
             w_conv_out=bf(w_conv_out), w_out=bf(w_out))
    ffn_g, ffn_u, ffn_d = bf(ffn_w_gate), bf(ffn_w_up), bf(ffn_w_down)
    moe_g, moe_u, moe_d = bf(moe_w_gate), bf(moe_w_up), bf(moe_w_down)

    rows8 = -(-(bt + 1) // 8) * 8
    cvec = jnp.zeros((rows8, d), F32).at[:bt].set(c).at[bt].set(c_ctx)
    mods = _adaln(cvec, w_ada, b_ada)

    tabs_x = _fft_tables(seq)
    tabs_c = _fft_tables(lc)
    x2 = x.reshape(bt * seq, d)
    c2 = ctx.reshape(bt * lc, d)
    for l in range(depth):
        last = l == depth - 1
        mod_x = mods[l, :bt].reshape(bt, N_MOD, d)
        mod_c = jnp.broadcast_to(mods[l, bt].reshape(1, N_MOD, d), (bt, N_MOD, d))
        bias = _bias_tables(na_rpb[l], seq // GRID_W)
        dense = l % 2 == 0
        j = l // 2
        if last:
            outs = _inproj(c2, mod_c, norm1_g[l], p["w_in"][l], tabs_c[3], lc)
            kv_ctx = (outs[3], outs[4])
        else:
            c_mid, kv_ctx = _token_mix(c2, mod_c, l, lc, bt, p, None, tabs_c, None)
            if not dense:
                raise NotImplementedError("MoE channel mixer on context tokens (only reached when depth > 2)")
            c2 = _ffn(c_mid, mod_c, norm2_g[l], ffn_g[j], ffn_u[j], ffn_d[j], final_g, lc, False)
        x_mid, _ = _token_mix(x2, mod_x, l, seq, bt, p, kv_ctx, tabs_x, bias)
        if dense:
            x2 = _ffn(x_mid, mod_x, norm2_g[l], ffn_g[j], ffn_u[j], ffn_d[j], final_g, seq, last)
        else:
            h_bf, route = _router(x_mid, mod_x, norm2_g[l], moe_router[j], seq)
            plan = _moe_plan(route, n_exp)
            hs = _moe_gather(h_bf, plan)
            y = _moe_experts(hs, moe_g[j], moe_u[j], moe_d[j], plan)
            x2 = _moe_combine(y, x_mid, mod_x, final_g, plan, seq, last)
    return x2.reshape(bt, seq, d)
```

```python
import functools

import numpy as np
import jax
import jax.numpy as jnp
from jax import lax
from jax.experimental import pallas as pl
from jax.experimental.pallas import tpu as pltpu

F32 = jnp.float32
BF16 = jnp.bfloat16
I32 = jnp.int32

EPS = 1e-6
N_MOD = 6
GRID_W = 64
WIN_H = 8
WIN_W = 16
F_GROUPS = 4
F_GROUP_DIM = 64
F_WIDTH = F_GROUPS * F_GROUP_DIM
NA_HEADS = 8
NA_HEAD_DIM = 64
NA_WIDTH = NA_HEADS * NA_HEAD_DIM
CONV_WIDTH = 256
CONV_K = 3
N_BRANCH = 3
TOP_K = 2
Q_OFF = F_WIDTH
K_OFF = Q_OFF + NA_WIDTH
V_OFF = K_OFF + NA_WIDTH
U_OFF = V_OFF + NA_WIDTH
B_OFF = U_OFF + CONV_WIDTH
C_OFF = B_OFF + CONV_WIDTH
G_OFF = C_OFF + CONV_WIDTH

LANES = 128
BF16_SUBLANES = 16
VMEM_LIMIT_BYTES = 56 * 1024 * 1024

ROW_TILE = 512
Q_ROWS = 8
KEY_ROWS = 16
MASK_VALUE = -1e30
MOE_EXPERT_TILE = 512
MOE_SLOT_TILE = 256
MOE_TOKEN_CHUNK = 512
MOE_FF_CHUNK = 256


def _params(sem):
    return pltpu.CompilerParams(dimension_semantics=sem, vmem_limit_bytes=VMEM_LIMIT_BYTES)


def _const_spec(shape):
    nd = len(shape)
    return pl.BlockSpec(shape, lambda *_: (0,) * nd, pipeline_mode=pl.Buffered(1))


def _dot(a, b):
    return jnp.dot(a, b, preferred_element_type=F32)


def _split_bf16(x):
    hi = x.astype(BF16)
    lo = (x - hi.astype(F32)).astype(BF16)
    return hi, lo


def _dot_split(a, w):
    ah, al = _split_bf16(a)
    wh, wl = _split_bf16(w)
    return _dot(ah, wh) + (_dot(ah, wl) + _dot(al, wh))


def _norm_mod(x, g, shift, scale):
    ms = jnp.mean(x * x, axis=-1, keepdims=True)
    y = x * lax.rsqrt(ms + EPS) * g
    return y * (1.0 + scale) + shift


def _rmsnorm(x, g):
    ms = jnp.mean(x * x, axis=-1, keepdims=True)
    return x * lax.rsqrt(ms + EPS) * g


def _adaln_kernel(c_ref, w_ref, b_ref, o_ref):
    a = c_ref[...]
    a = a * jax.nn.sigmoid(a)
    o_ref[0] = _dot_split(a, w_ref[0]) + b_ref[0]


def _adaln(cvec, w_ada, b_ada):
    depth, d, width = w_ada.shape
    rows = cvec.shape[0]
    tn = width // 4
    return pl.pallas_call(
        _adaln_kernel,
        grid=(depth, width // tn),
        in_specs=[
            pl.BlockSpec((rows, d), lambda l, j: (0, 0)),
            pl.BlockSpec((1, d, tn), lambda l, j: (l, 0, j)),
            pl.BlockSpec((1, 1, tn), lambda l, j: (l, 0, j)),
        ],
        out_specs=pl.BlockSpec((1, rows, tn), lambda l, j: (l, 0, j)),
        out_shape=jax.ShapeDtypeStruct((depth, rows, width), F32),
        compiler_params=_params(("arbitrary", "arbitrary")),
        name="adaln",
    )(cvec, w_ada, b_ada.reshape(depth, 1, width))


def _inproj_kernel(x_ref, mod_ref, g_ref, w_ref, bd_ref,
                   a0_ref, b0_ref, q_ref, k_ref, v_ref, u_ref, b_ref, c_ref, gate_ref, *, d):
    h = _norm_mod(x_ref[...], g_ref[...], mod_ref[0, 0:1, :], mod_ref[0, 1:2, :]).astype(BF16)

    def proj(c0, c1):
        return _dot(h, w_ref[:, c0:c1])

    zf = proj(0, Q_OFF).astype(BF16)
    ab = _dot(zf, bd_ref[...])
    a0_ref[...] = ab[:, :F_WIDTH].astype(BF16)
    b0_ref[...] = ab[:, F_WIDTH:].astype(BF16)
    q_ref[...] = (proj(Q_OFF, K_OFF) * (NA_HEAD_DIM ** -0.5)).astype(BF16)
    k_ref[...] = proj(K_OFF, V_OFF).astype(BF16)
    v_ref[...] = proj(V_OFF, U_OFF).astype(BF16)
    u_ref[...] = proj(U_OFF, B_OFF).astype(BF16)
    b_ref[...] = proj(B_OFF, C_OFF).astype(BF16)
    c_ref[...] = proj(C_OFF, G_OFF).astype(BF16)
    step = 512 if d % 512 == 0 else d
    for j in range(0, N_BRANCH * d, step):
        gate_ref[:, j:j + step] = jax.nn.sigmoid(proj(G_OFF + j, G_OFF + j + step)).astype(BF16)


def _inproj(x2, mod, g, w_bf, bd, seq_len):
    n, d = x2.shape
    tm = min(ROW_TILE, seq_len)
    pw = w_bf.shape[1]
    widths = [F_WIDTH, F_WIDTH, NA_WIDTH, NA_WIDTH, NA_WIDTH, CONV_WIDTH, CONV_WIDTH, CONV_WIDTH, N_BRANCH * d]
    row = lambda w: pl.BlockSpec((tm, w), lambda i: (i, 0))
    return pl.pallas_call(
        functools.partial(_inproj_kernel, d=d),
        grid=(n // tm,),
        in_specs=[
            row(d),
            pl.BlockSpec((1, N_MOD, d), lambda i: (i * tm // seq_len, 0, 0)),
            _const_spec((1, d)),
            _const_spec((d, pw)),
            _const_spec((F_WIDTH, 2 * F_WIDTH)),
        ],
        out_specs=[row(w) for w in widths],
        out_shape=[jax.ShapeDtypeStruct((n, w), BF16) for w in widths],
        compiler_params=_params(("parallel",)),
        name="inproj",
    )(x2, mod, g.reshape(1, d), w_bf, bd)


def _fft_split(seq_len):
    b = 64 if seq_len >= 1024 else 16
    return seq_len // b, b


def _fft_tables(seq_len):
    a_pts, b_pts = _fft_split(seq_len)
    ia = np.arange(a_pts)
    ang = 2.0 * np.pi * ((ia[:, None] * ia[None, :]) % a_pts) / a_pts
    c, s = np.cos(ang) / np.sqrt(a_pts), np.sin(ang) / np.sqrt(a_pts)
    w1 = np.block([[c, -s], [-s, -c]])
    ib = np.arange(b_pts)
    num = (ib[None, :, None] * ib[None, None, :] * a_pts + ib[None, None, :] * ia[:, None, None]) % seq_len
    th = 2.0 * np.pi * num / seq_len
    mc, ms = np.cos(th) / np.sqrt(b_pts), np.sin(th) / np.sqrt(b_pts)
    ic = np.arange(F_GROUP_DIM)
    angc = 2.0 * np.pi * ((ic[:, None] * ic[None, :]) % F_GROUP_DIM) / F_GROUP_DIM
    eye = np.eye(F_GROUPS)
    bd = np.concatenate([np.kron(eye, np.cos(angc)), np.kron(eye, np.sin(angc))], axis=1) / np.sqrt(F_GROUP_DIM)
    return tuple(jnp.asarray(t, F32).astype(BF16) for t in (w1, mc, ms, bd))


def _fft1_kernel(a_ref, b_ref, w_ref, o_ref, *, a_pts):
    x = jnp.concatenate([a_ref[0], b_ref[0]], axis=0)
    t = _dot(w_ref[...], x)
    o_ref[0, 0] = t[:a_pts].astype(BF16)
    o_ref[0, 1] = t[a_pts:].astype(BF16)


def _fft2_kernel(tr_ref, ti_ref, mc_ref, ms_ref, o_ref, *, ac):
    for j in range(ac):
        y = _dot(mc_ref[j], tr_ref[0, 0, j]) + _dot(ms_ref[j], ti_ref[0, 0, j])
        o_ref[0, j] = y.astype(BF16)


def _fourier(a0, b0, bt, seq_len, w1, mc, ms):
    a_pts, b_pts = _fft_split(seq_len)
    c = F_WIDTH
    bc = b_pts * c
    cw = min(2048, bc)
    view = lambda z: z.reshape(bt, a_pts, bc)
    t = pl.pallas_call(
        functools.partial(_fft1_kernel, a_pts=a_pts),
        grid=(bt, bc // cw),
        in_specs=[
            pl.BlockSpec((1, a_pts, cw), lambda b, j: (b, 0, j)),
            pl.BlockSpec((1, a_pts, cw), lambda b, j: (b, 0, j)),
            _const_spec((2 * a_pts, 2 * a_pts)),
        ],
        out_specs=pl.BlockSpec((1, 2, a_pts, cw), lambda b, j: (b, 0, 0, j)),
        out_shape=jax.ShapeDtypeStruct((bt, 2, a_pts, bc), BF16),
        compiler_params=_params(("parallel", "parallel")),
        name="fft1",
    )(view(a0), view(b0), w1)
    t5 = t.reshape(bt, 2, a_pts, b_pts, c)
    ac = min(a_pts, 16)
    out = pl.pallas_call(
        functools.partial(_fft2_kernel, ac=ac),
        grid=(a_pts // ac, bt),
        in_specs=[
            pl.BlockSpec((1, 1, ac, b_pts, c), lambda i, b: (b, 0, i, 0, 0)),
            pl.BlockSpec((1, 1, ac, b_pts, c), lambda i, b: (b, 1, i, 0, 0)),
            pl.BlockSpec((ac, b_pts, b_pts), lambda i, b: (i, 0, 0)),
            pl.BlockSpec((ac, b_pts, b_pts), lambda i, b: (i, 0, 0)),
        ],
        out_specs=pl.BlockSpec((1, ac, b_pts, c), lambda i, b: (b, i, 0, 0)),
        out_shape=jax.ShapeDtypeStruct((bt, a_pts, b_pts, c), BF16),
        compiler_params=_params(("parallel", "parallel")),
        name="fft2",
    )(t5, t5, mc, ms)
    return out.reshape(bt * a_pts, bc)


def _window_rows(rows):
    r0s = np.array([0, Q_ROWS, rows - Q_ROWS])
    i = np.arange(Q_ROWS)
    j = np.arange(KEY_ROWS)
    r = r0s[:, None, None] + i[None, :, None]
    kr = r0s[:, None, None] - WIN_H // 2 + j[None, None, :]
    base = np.clip(r - WIN_H // 2, 0, rows - WIN_H)
    ok = (kr >= 0) & (kr < rows) & (kr >= base) & (kr < base + WIN_H)
    return tuple(tuple(tuple(bool(b) for b in row) for row in blk) for blk in ok)


def _biastab_kernel(rpb_ref, o_ref, *, ok_r):
    w = GRID_W
    cq = lax.broadcasted_iota(I32, (w, LANES), 0)
    kc = lax.broadcasted_iota(I32, (w, LANES), 1)
    cs = jnp.clip(cq - WIN_W // 2, 0, w - WIN_W)
    ok_c = (kc >= cs) & (kc < cs + WIN_W)
    low = kc < w
    masked = jnp.full((w, LANES), MASK_VALUE, F32)
    lo, hi = [], []
    for dr in range(2 * WIN_H - 1):
        xb = jnp.broadcast_to(rpb_ref[0, dr:dr + 1, :], (w, LANES))
        t = pltpu.roll(xb, LANES - (WIN_W - 1), 1, stride=1, stride_axis=0)
        t = jnp.where(ok_c, t, MASK_VALUE)
        lo.append(t)
        hi.append(pltpu.roll(t, w, 1))
    for v in range(3):
        for i in range(Q_ROWS):
            for jj in range(KEY_ROWS // 2):
                j0 = 2 * jj
                d0 = j0 - i + WIN_H // 2 - 1
                a = lo[d0] if ok_r[v][i][j0] else masked
                b = hi[d0 + 1] if ok_r[v][i][j0 + 1] else masked
                blk = masked if (a is masked and b is masked) else jnp.where(low, a, b)
                o_ref[v, 0, i * w:(i + 1) * w, jj * LANES:(jj + 1) * LANES] = blk


def _bias_tables(rpb, rows):
    n_dr, n_dc = rpb.shape[1], rpb.shape[2]
    rp = jnp.zeros((NA_HEADS, 2 * WIN_H, LANES), F32).at[:, :n_dr, :n_dc].set(rpb.astype(F32))
    qb, wk = Q_ROWS * GRID_W, KEY_ROWS * GRID_W
    return pl.pallas_call(
        functools.partial(_biastab_kernel, ok_r=_window_rows(rows)),
        grid=(NA_HEADS,),
        in_specs=[pl.BlockSpec((1, 2 * WIN_H, LANES), lambda h: (h, 0, 0))],
        out_specs=pl.BlockSpec((3, 1, qb, wk), lambda h: (0, h, 0, 0)),
        out_shape=jax.ShapeDtypeStruct((3, NA_HEADS, qb, wk), F32),
        compiler_params=_params(("parallel",)),
        name="biastab",
    )(rp)


def _softmax_pv(s_parts, v_parts):
    m = s_parts[0].max(axis=-1, keepdims=True)
    for s in s_parts[1:]:
        m = jnp.maximum(m, s.max(axis=-1, keepdims=True))
    l = None
    pv = None
    for s, v in zip(s_parts, v_parts):
        p = jnp.exp(s - m)
        ls = p.sum(axis=-1, keepdims=True)
        o = _dot(p.astype(BF16), v)
        l = ls if l is None else l + ls
        pv = o if pv is None else pv + o
    return pv * (1.0 / l)


def _natten_kernel(q_ref, kp_ref, kc_ref, kn_ref, vp_ref, vc_ref, vn_ref, kx_ref, vx_ref, bias_ref,
                   o_ref, kw_ref, vw_ref, *, lc):
    qb = Q_ROWS * GRID_W
    halo = (KEY_ROWS - Q_ROWS) // 2 * GRID_W
    wk = KEY_ROWS * GRID_W
    for dst, (p, c, n, x) in ((kw_ref, (kp_ref, kc_ref, kn_ref, kx_ref)), (vw_ref, (vp_ref, vc_ref, vn_ref, vx_ref))):
        dst[0:halo] = p[qb - halo:qb]
        dst[halo:halo + qb] = c[...]
        dst[halo + qb:wk] = n[0:halo]
        dst[wk:wk + lc] = x[...]
    hq = qb // 2
    lane = lax.broadcasted_iota(I32, (hq, LANES), 1)
    low = lane < NA_HEAD_DIM
    nt = (((1,), (1,)), ((), ()))
    for hp in range(NA_HEADS // 2):
        sl = slice(LANES * hp, LANES * (hp + 1))
        kw = kw_ref[:, sl]
        vw = vw_ref[:, sl]
        for half in range(2):
            rows = slice(half * hq, (half + 1) * hq)
            cols = slice(half * halo, half * halo + wk - halo)
            qp = q_ref[rows, sl]
            outs = []
            for sub in range(2):
                qm = jnp.where(low if sub == 0 else jnp.logical_not(low), qp, jnp.zeros_like(qp))
                s_win = lax.dot_general(qm, kw[cols], nt, preferred_element_type=F32)
                s_win = s_win + bias_ref[0, 2 * hp + sub, rows, cols]
                s_ctx = lax.dot_general(qm, kw[wk:], nt, preferred_element_type=F32)
                outs.append(_softmax_pv([s_win, s_ctx], [vw[cols], vw[wk:]]))
            o_ref[rows, sl] = jnp.where(low, outs[0], outs[1]).astype(BF16)


def _natten(q, k, v, kx, vx, bias, bt, seq_len, lc):
    n = q.shape[0]
    qb = Q_ROWS * GRID_W
    nb = seq_len // qb
    assert seq_len % qb == 0 and nb >= 2, "neighbourhood attention kernel needs >= 16 grid rows, a multiple of 8"
    wk = KEY_ROWS * GRID_W
    cur = lambda b, r: (b * nb + r, 0)
    prev = lambda b, r: (b * nb + jnp.maximum(r - 1, 0), 0)
    nxt = lambda b, r: (b * nb + jnp.minimum(r + 1, nb - 1), 0)
    blk = lambda im: pl.BlockSpec((qb, NA_WIDTH), im)
    variant = lambda b, r: (jnp.where(r == 0, 0, jnp.where(r == nb - 1, 2, 1)), 0, 0, 0)
    return pl.pallas_call(
        functools.partial(_natten_kernel, lc=lc),
        grid=(bt, nb),
        in_specs=[
            blk(cur), blk(prev), blk(cur), blk(nxt), blk(prev), blk(cur), blk(nxt),
            pl.BlockSpec((lc, NA_WIDTH), lambda b, r: (b, 0)),
            pl.BlockSpec((lc, NA_WIDTH), lambda b, r: (b, 0)),
            pl.BlockSpec((1, NA_HEADS, qb, wk), variant, pipeline_mode=pl.Buffered(1)),
        ],
        out_specs=blk(cur),
        out_shape=jax.ShapeDtypeStruct((n, NA_WIDTH), BF16),
        scratch_shapes=[pltpu.VMEM((wk + lc, NA_WIDTH), BF16), pltpu.VMEM((wk + lc, NA_WIDTH), BF16)],
        compiler_params=_params(("parallel", "arbitrary")),
        name="natten",
    )(q, k, k, k, v, v, v, kx, vx, bias)


def _ctxattn_kernel(q_ref, k_ref, v_ref, o_ref):
    lc = q_ref.shape[0]
    lane = lax.broadcasted_iota(I32, (lc, LANES), 1)
    low = lane < NA_HEAD_DIM
    for hp in range(NA_HEADS // 2):
        sl = slice(LANES * hp, LANES * (hp + 1))
        qp = q_ref[:, sl]
        kw = k_ref[:, sl]
        vw = v_ref[:, sl]
        outs = []
        for sub in range(2):
            qm = jnp.where(low if sub == 0 else jnp.logical_not(low), qp, jnp.zeros_like(qp))
            s = lax.dot_general(qm, kw, (((1,), (1,)), ((), ())), preferred_element_type=F32)
            outs.append(_softmax_pv([s], [vw]))
        o_ref[:, sl] = jnp.where(low, outs[0], outs[1]).astype(BF16)


def _ctxattn(q, k, v, bt, lc):
    blk = pl.BlockSpec((lc, NA_WIDTH), lambda b: (b, 0))
    return pl.pallas_call(
        _ctxattn_kernel,
        grid=(bt,),
        in_specs=[blk, blk, blk],
        out_specs=blk,
        out_shape=jax.ShapeDtypeStruct(q.shape, BF16),
        compiler_params=_params(("parallel",)),
        name="ctxattn",
    )(q, k, v)


def _merge_kernel(fm_ref, at_ref, u_ref, b_ref, c_ref, up_ref, cp_ref, un_ref, cn_ref, gate_ref, x_ref, mod_ref,
                  cw_ref, wf_ref, wa_ref, wc_ref, wo_ref, o_ref, pre_ref, *, tm, d, a_pts, seq_len):
    i = pl.program_id(0)
    t0 = i * tm
    first = (t0 % seq_len) == 0
    last = ((t0 + tm) % seq_len) == 0
    p = c_ref[...].astype(F32) * u_ref[...].astype(F32)
    hs = BF16_SUBLANES
    p_above = cp_ref[hs - 1:hs, :].astype(F32) * up_ref[hs - 1:hs, :].astype(F32)
    p_below = cn_ref[0:1, :].astype(F32) * un_ref[0:1, :].astype(F32)
    p_above = jnp.where(first, 0.0, p_above)
    p_below = jnp.where(last, 0.0, p_below)
    row = lax.broadcasted_iota(I32, (tm, CONV_WIDTH), 0)
    p_dn = jnp.where(row == 0, p_above, pltpu.roll(p, 1, 0))
    p_up = jnp.where(row == tm - 1, p_below, pltpu.roll(p, tm - 1, 0))
    cv = cw_ref[0:1, :] * p_dn + cw_ref[1:2, :] * p + cw_ref[2:3, :] * p_up
    y_c = _dot((b_ref[...].astype(F32) * cv).astype(BF16), wc_ref[...])
    y_a = _dot(at_ref[...], wa_ref[...])
    pre_ref[...] = (gate_ref[:, d:2 * d].astype(F32) * y_a + gate_ref[:, 2 * d:3 * d].astype(F32) * y_c)
    for j in range(tm // a_pts):
        rs = slice(j * a_pts, (j + 1) * a_pts)
        y_f = _dot(fm_ref[:, j * F_WIDTH:(j + 1) * F_WIDTH], wf_ref[...])
        pre_ref[rs, :] += gate_ref[rs, 0:d].astype(F32) * y_f
    mix = _dot(pre_ref[...].astype(BF16), wo_ref[...])
    o_ref[...] = x_ref[...] + mod_ref[0, 2:3, :] * mix


def _merge(fm, attn, zu, zb, zc, gates, x2, mod, conv_w, wf, wa, wc, wo, seq_len):
    n, d = x2.shape
    tm = min(ROW_TILE, seq_len)
    a_pts, b_pts = _fft_split(seq_len)
    assert tm % a_pts == 0
    hs = BF16_SUBLANES
    nh = n // hs
    per = tm // hs
    row = lambda w: pl.BlockSpec((tm, w), lambda i: (i, 0))
    above = pl.BlockSpec((hs, CONV_WIDTH), lambda i: (jnp.maximum(i * per - 1, 0), 0))
    below = pl.BlockSpec((hs, CONV_WIDTH), lambda i: (jnp.minimum((i + 1) * per, nh - 1), 0))
    tiles_per_seq = seq_len // tm
    sub = tm // a_pts
    fm_spec = pl.BlockSpec((a_pts, sub * F_WIDTH), lambda i: (i // tiles_per_seq, i % tiles_per_seq))
    return pl.pallas_call(
        functools.partial(_merge_kernel, tm=tm, d=d, a_pts=a_pts, seq_len=seq_len),
        grid=(n // tm,),
        in_specs=[
            fm_spec, row(NA_WIDTH), row(CONV_WIDTH), row(CONV_WIDTH), row(CONV_WIDTH),
            above, above, below, below,
            row(N_BRANCH * d), row(d),
            pl.BlockSpec((1, N_MOD, d), lambda i: (i * tm // seq_len, 0, 0)),
            _const_spec((CONV_K, CONV_WIDTH)),
            _const_spec((F_WIDTH, d)), _const_spec((NA_WIDTH, d)), _const_spec((CONV_WIDTH, d)), _const_spec((d, d)),
        ],
        out_specs=row(d),
        out_shape=jax.ShapeDtypeStruct((n, d), F32),
        scratch_shapes=[pltpu.VMEM((tm, d), F32)],
        compiler_params=_params(("parallel",)),
        name="merge",
    )(fm, attn, zu, zb, zc, zu, zc, zu, zc, gates, x2, mod, conv_w, wf, wa, wc, wo)


def _ffn_kernel(x_ref, mod_ref, g_ref, wg_ref, wu_ref, wd_ref, fg_ref, o_ref, acc_ref, *, tf, final):
    x = x_ref[...]
    h = _norm_mod(x, g_ref[...], mod_ref[0, 3:4, :], mod_ref[0, 4:5, :]).astype(BF16)
    ff = wg_ref.shape[1]
    for f in range(0, ff, tf):
        g = _dot(h, wg_ref[:, f:f + tf])
        u = _dot(h, wu_ref[:, f:f + tf])
        a = (g * jax.nn.sigmoid(g) * u).astype(BF16)
        y = _dot(a, wd_ref[f:f + tf, :])
        if f == 0:
            acc_ref[...] = y
        else:
            acc_ref[...] += y
    out = x + mod_ref[0, 5:6, :] * acc_ref[...]
    o_ref[...] = _rmsnorm(out, fg_ref[...]) if final else out


def _ffn(x2, mod, g, wg, wu, wd, final_g, seq_len, final):
    n, d = x2.shape
    ff = wg.shape[1]
    tm = min(ROW_TILE, seq_len)
    tf = 256 if ff % 256 == 0 else LANES
    row = pl.BlockSpec((tm, d), lambda i: (i, 0))
    return pl.pallas_call(
        functools.partial(_ffn_kernel, tf=tf, final=final),
        grid=(n // tm,),
        in_specs=[
            row,
            pl.BlockSpec((1, N_MOD, d), lambda i: (i * tm // seq_len, 0, 0)),
            _const_spec((1, d)), _const_spec((d, ff)), _const_spec((d, ff)), _const_spec((ff, d)),
            _const_spec((1, d)),
        ],
        out_specs=row,
        out_shape=jax.ShapeDtypeStruct((n, d), F32),
        scratch_shapes=[pltpu.VMEM((tm, d), F32)],
        compiler_params=_params(("parallel",)),
        name="ffn",
    )(x2, mod, g.reshape(1, d), wg, wu, wd, final_g.reshape(1, d))


def _router_kernel(x_ref, mod_ref, g_ref, wr_ref, h_ref, r_ref, *, n_exp):
    h = _norm_mod(x_ref[...], g_ref[...], mod_ref[0, 3:4, :], mod_ref[0, 4:5, :])
    h_ref[...] = h.astype(BF16)
    logits = _dot_split(h, wr_ref[...])
    lane = lax.broadcasted_iota(I32, logits.shape, 1)
    lane_f = lane.astype(F32)
    neg = -jnp.inf
    l1 = jnp.where(lane < n_exp, logits, neg)
    m1 = l1.max(axis=-1, keepdims=True)
    i1 = jnp.where(l1 == m1, lane_f, float(LANES)).min(axis=-1, keepdims=True)
    l2 = jnp.where(lane_f == i1, neg, l1)
    m2 = l2.max(axis=-1, keepdims=True)
    i2 = jnp.where(l2 == m2, lane_f, float(LANES)).min(axis=-1, keepdims=True)
    e = jnp.exp(m2 - m1)
    w1 = 1.0 / (1.0 + e)
    w2 = e * w1
    out = jnp.where(lane == 0, i1, 0.0)
    out = jnp.where(lane == 1, i2, out)
    out = jnp.where(lane == 2, w1, out)
    out = jnp.where(lane == 3, w2, out)
    r_ref[...] = out


def _router(x2, mod, g, w_router, seq_len):
    n, d = x2.shape
    n_exp = w_router.shape[1]
    tm = min(ROW_TILE, seq_len)
    wr = jnp.zeros((d, LANES), F32).at[:, :n_exp].set(w_router)
    row = lambda w: pl.BlockSpec((tm, w), lambda i: (i, 0))
    return pl.pallas_call(
        functools.partial(_router_kernel, n_exp=n_exp),
        grid=(n // tm,),
        in_specs=[
            row(d),
            pl.BlockSpec((1, N_MOD, d), lambda i: (i * tm // seq_len, 0, 0)),
            _const_spec((1, d)), _const_spec((d, LANES)),
        ],
        out_specs=[row(d), row(LANES)],
        out_shape=[jax.ShapeDtypeStruct((n, d), BF16), jax.ShapeDtypeStruct((n, LANES), F32)],
        compiler_params=_params(("parallel",)),
        name="router",
    )(x2, mod, g.reshape(1, d), wr)


def _count_le(sorted_vals, v):
    return jnp.sum((sorted_vals[None, :] <= v[:, None]).astype(I32), axis=1)


def _moe_plan(route, n_exp):
    n = route.shape[0]
    te, tg, tc = MOE_EXPERT_TILE, MOE_SLOT_TILE, MOE_TOKEN_CHUNK
    assert n % tc == 0 and te % tg == 0
    idx = route[:, :TOP_K].astype(I32)
    wts = route[:, TOP_K:2 * TOP_K]
    ex = jnp.arange(n_exp, dtype=I32)
    hit = idx[:, :, None] == ex[None, None, :]
    sel = hit.any(axis=1)
    gate = jnp.sum(jnp.where(hit, wts[:, :, None], 0.0), axis=1)
    seli = sel.astype(I32)
    cnt = seli.sum(axis=0)
    padded = (cnt + te - 1) // te * te
    ends = jnp.cumsum(padded)
    off = ends - padded
    total = ends[-1]
    rank = jnp.cumsum(seli, axis=0) - seli
    pos = jnp.where(sel, off[None, :] + rank, -1)
    n_slots = TOP_K * n + n_exp * te
    n_tg, n_te, n_ch = n_slots // tg, n_slots // te, n // tc
    g_max = n_exp * n_ch + n_tg

    posc = pos.reshape(n_ch, tc, n_exp)
    hasc = (posc >= 0).any(axis=1)
    pmin = jnp.where(posc >= 0, posc, n_slots).min(axis=1)
    pmax = posc.max(axis=1)

    te_base = jnp.arange(n_te, dtype=I32) * te
    te_valid = (te_base < total).astype(I32)
    last_valid = jnp.maximum(total // te - 1, 0)
    te_exp = jnp.minimum(_count_le(ends, jnp.minimum(te_base, last_valid * te)), n_exp - 1).astype(I32)

    tg_base = jnp.arange(n_tg, dtype=I32) * tg
    tg_exp = jnp.minimum(_count_le(ends, tg_base), n_exp - 1).astype(I32)
    used_end = off[tg_exp] + cnt[tg_exp]
    has = tg_base < used_end
    ch = jnp.arange(n_ch, dtype=I32)[None, :]
    has_e = jnp.take(hasc.T, tg_exp, axis=0)
    first_c = jnp.where(has_e & (jnp.take(pmax.T, tg_exp, axis=0) >= tg_base[:, None]), ch, n_ch).min(axis=1)
    last_c = jnp.where(has_e & (jnp.take(pmin.T, tg_exp, axis=0) < tg_base[:, None] + tg), ch, -1).max(axis=1)
    c_lo = jnp.where(has, first_c, 0)
    c_hi = jnp.where(has, last_c, 0)
    n_it = c_hi - c_lo + 1
    start = jnp.cumsum(n_it) - n_it
    tot_g = start[-1] + n_it[-1]
    g = jnp.arange(g_max, dtype=I32)
    gc = jnp.minimum(g, tot_g - 1)
    s_of = (_count_le(start, gc) - 1).astype(I32)
    ga_valid = (g < tot_g).astype(I32)
    ga_tile = s_of
    ga_chunk = (c_lo[s_of] + gc - start[s_of]).astype(I32)
    ga_first = ((gc == start[s_of]) & (g < tot_g)).astype(I32)

    s_lo = jnp.where(hasc, pmin // tg, 0).reshape(-1)
    s_hi = jnp.where(hasc, pmax // tg, 0).reshape(-1)
    ccnt = jnp.where(hasc.reshape(-1), s_hi - s_lo + 1, 0)
    cend = jnp.cumsum(ccnt)
    cstart = cend - ccnt
    tot_c = cend[-1]
    hc = jnp.minimum(g, tot_c - 1)
    pair = (_count_le(cstart, hc) - 1).astype(I32)
    co_valid = (g < tot_c).astype(I32)
    co_chunk = (pair // n_exp).astype(I32)
    co_exp = (pair % n_exp).astype(I32)
    co_tile = (s_lo[pair] + hc - cstart[pair]).astype(I32)
    chunk_start = cstart.reshape(n_ch, n_exp)[:, 0]
    chunk_end = cend.reshape(n_ch, n_exp)[:, -1]
    co_first = ((hc == chunk_start[co_chunk]) & (g < tot_c)).astype(I32)
    co_last = ((hc == chunk_end[co_chunk] - 1) & (g < tot_c)).astype(I32)

    pos_t = pos.T.astype(I32)
    return dict(n_slots=n_slots, g_max=g_max, te_exp=te_exp, te_valid=te_valid, tg_exp=tg_exp,
                ga=(ga_tile, ga_chunk, ga_first, ga_valid), co=(co_tile, co_chunk, co_exp, co_first, co_last, co_valid),
                pos_t=pos_t, pos_f=pos.astype(F32), gate=gate)


def _gather_kernel(it_tile, it_chunk, it_first, it_valid, tg_exp, h_ref, pos_ref, o_ref):
    g = pl.program_id(0)
    tg, tc = MOE_SLOT_TILE, MOE_TOKEN_CHUNK

    @pl.when(it_valid[g] == 1)
    def _():
        s = it_tile[g]
        e = tg_exp[s]
        pos_row = pos_ref[pl.ds(e, 1), :]
        slot = s * tg + lax.broadcasted_iota(I32, (tg, tc), 0)
        onehot = jnp.where(pos_row == slot, 1.0, 0.0).astype(BF16)
        r = _dot(onehot, h_ref[...]).astype(BF16)

        @pl.when(it_first[g] == 1)
        def _():
            o_ref[...] = r

        @pl.when(it_first[g] == 0)
        def _():
            o_ref[...] += r


def _moe_gather(h_bf, plan):
    n, d = h_bf.shape
    n_exp = plan["pos_t"].shape[0]
    tg, tc = MOE_SLOT_TILE, MOE_TOKEN_CHUNK
    grid_spec = pltpu.PrefetchScalarGridSpec(
        num_scalar_prefetch=5,
        grid=(plan["g_max"],),
        in_specs=[
            pl.BlockSpec((tc, d), lambda g, t, c, f, v, te: (c[g], 0)),
            pl.BlockSpec((n_exp, tc), lambda g, t, c, f, v, te: (0, c[g])),
        ],
        out_specs=pl.BlockSpec((tg, d), lambda g, t, c, f, v, te: (t[g], 0)),
    )
    return pl.pallas_call(
        _gather_kernel,
        grid_spec=grid_spec,
        out_shape=jax.ShapeDtypeStruct((plan["n_slots"], d), BF16),
        compiler_params=_params(("arbitrary",)),
        name="moe_gather",
    )(*plan["ga"], plan["tg_exp"], h_bf, plan["pos_t"])


def _experts_kernel(te_exp, te_valid, h_ref, wg_ref, wu_ref, wd_ref, y_ref, acc_ref, *, sub):
    i = pl.program_id(0)
    f = pl.program_id(1)

    @pl.when(f == 0)
    def _():
        acc_ref[...] = jnp.zeros_like(acc_ref)

    @pl.when(te_valid[i] == 1)
    def _():
        h = h_ref[...]
        for c in range(0, wg_ref.shape[2], sub):
            g = _dot(h, wg_ref[0, :, c:c + sub])
            u = _dot(h, wu_ref[0, :, c:c + sub])
            a = (g * jax.nn.sigmoid(g) * u).astype(BF16)
            acc_ref[...] += _dot(a, wd_ref[0, c:c + sub, :])

    @pl.when(f == pl.num_programs(1) - 1)
    def _():
        y_ref[...] = acc_ref[...].astype(BF16)


def _moe_experts(hs, wg, wu, wd, plan):
    n_slots, d = hs.shape
    ff = wg.shape[2]
    te = MOE_EXPERT_TILE
    sub = MOE_FF_CHUNK if ff % (2 * MOE_FF_CHUNK) == 0 else LANES
    tf = ff // 2 if (ff // 2) % sub == 0 else ff
    grid_spec = pltpu.PrefetchScalarGridSpec(
        num_scalar_prefetch=2,
        grid=(n_slots // te, ff // tf),
        in_specs=[
            pl.BlockSpec((te, d), lambda i, f, e, v: (i, 0)),
            pl.BlockSpec((1, d, tf), lambda i, f, e, v: (e[i], 0, f * v[i])),
            pl.BlockSpec((1, d, tf), lambda i, f, e, v: (e[i], 0, f * v[i])),
            pl.BlockSpec((1, tf, d), lambda i, f, e, v: (e[i], f * v[i], 0)),
        ],
        out_specs=pl.BlockSpec((te, d), lambda i, f, e, v: (i, 0)),
        scratch_shapes=[pltpu.VMEM((te, d), F32)],
    )
    return pl.pallas_call(
        functools.partial(_experts_kernel, sub=sub),
        grid_spec=grid_spec,
        out_shape=jax.ShapeDtypeStruct((n_slots, d), BF16),
        compiler_params=_params(("arbitrary", "arbitrary")),
        name="moe_experts",
    )(plan["te_exp"], plan["te_valid"], hs, wg, wu, wd)


def _combine_kernel(it_tile, it_chunk, it_exp, it_first, it_last, it_valid,
                    y_ref, pos_ref, gate_ref, x_ref, mod_ref, fg_ref, o_ref, acc_ref, *, final):
    g = pl.program_id(0)
    tg, tc = MOE_SLOT_TILE, MOE_TOKEN_CHUNK

    @pl.when(it_first[g] == 1)
    def _():
        acc_ref[...] = jnp.zeros_like(acc_ref)

    @pl.when(it_valid[g] == 1)
    def _():
        e = it_exp[g]
        lane = lax.broadcasted_iota(I32, pos_ref.shape, 1)
        pos_col = jnp.sum(jnp.where(lane == e, pos_ref[...], 0.0), axis=-1, keepdims=True)
        w_col = jnp.sum(jnp.where(lane == e, gate_ref[...], 0.0), axis=-1, keepdims=True)
        slot = (it_tile[g] * tg + lax.broadcasted_iota(I32, (tc, tg), 1)).astype(F32)
        onehot = jnp.where(pos_col == slot, 1.0, 0.0).astype(BF16)
        acc_ref[...] += w_col * _dot(onehot, y_ref[...])

    @pl.when(it_last[g] == 1)
    def _():
        out = x_ref[...] + mod_ref[0, 5:6, :] * acc_ref[...]
        o_ref[...] = _rmsnorm(out, fg_ref[...]) if final else out


def _moe_combine(y, x2, mod, final_g, plan, seq_len, final):
    n, d = x2.shape
    n_exp = plan["pos_t"].shape[0]
    tg, tc = MOE_SLOT_TILE, MOE_TOKEN_CHUNK
    assert seq_len % tc == 0
    grid_spec = pltpu.PrefetchScalarGridSpec(
        num_scalar_prefetch=6,
        grid=(plan["g_max"],),
        in_specs=[
            pl.BlockSpec((tg, d), lambda g, t, c, *_: (t[g], 0)),
            pl.BlockSpec((tc, n_exp), lambda g, t, c, *_: (c[g], 0)),
            pl.BlockSpec((tc, n_exp), lambda g, t, c, *_: (c[g], 0)),
            pl.BlockSpec((tc, d), lambda g, t, c, *_: (c[g], 0)),
            pl.BlockSpec((1, N_MOD, d), lambda g, t, c, *_: (c[g] * tc // seq_len, 0, 0)),
            pl.BlockSpec((1, d), lambda g, *_: (0, 0)),
        ],
        out_specs=pl.BlockSpec((tc, d), lambda g, t, c, *_: (c[g], 0)),
        scratch_shapes=[pltpu.VMEM((tc, d), F32)],
    )
    return pl.pallas_call(
        functools.partial(_combine_kernel, final=final),
        grid_spec=grid_spec,
        out_shape=jax.ShapeDtypeStruct((n, d), F32),
        compiler_params=_params(("arbitrary",)),
        name="moe_combine",
    )(*plan["co"], y, plan["pos_f"], plan["gate"], x2, mod, final_g.reshape(1, d))


def _token_mix(x2, mod, l, seq_len, bt, p, kv_ctx, tabs, bias):
    a0, b0, q, k, v, zu, zb, zc, gates = _inproj(x2, mod, p["norm1_g"][l], p["w_in"][l], tabs[3], seq_len)
    fm = _fourier(a0, b0, bt, seq_len, *tabs[:3])
    if kv_ctx is None:
        attn = _ctxattn(q, k, v, bt, seq_len)
    else:
        attn = _natten(q, k, v, kv_ctx[0], kv_ctx[1], bias, bt, seq_len, kv_ctx[0].shape[0] // bt)
    out = _merge(fm, attn, zu, zb, zc, gates, x2, mod, p["conv_w"][l], p["w_fourier"][l], p["w_na"][l],
                 p["w_conv_out"][l], p["w_out"][l], seq_len)
    return out, (k, v)


def kernel(x, c, ctx, c_ctx, norm1_g, norm2_g, w_ada, b_ada, w_in, conv_w, na_rpb, w_fourier, w_na, w_conv_out, w_out, ffn_w_gate, ffn_w_up, ffn_w_down, moe_router, moe_w_gate, moe_w_up, moe_w_down, final_g):
    bt, seq, d = x.shape
    lc = ctx.shape[1]
    depth = w_in.shape[0]
    n_exp = moe_router.shape[-1]
    bf = lambda w: w.astype(BF16)
    p = dict(norm1_g=norm1_g, w_in=bf(w_in), conv_w=conv_w, w_fourier=bf(w_fourier), w_na=bf(w_na),
             w_conv_out=bf(w_conv_out), w_out=bf(w_out))
    ffn_g, ffn_u, ffn_d = bf(ffn_w_gate), bf(ffn_w_up), bf(ffn_w_down)
    moe_g, moe_u, moe_d = bf(moe_w_gate), bf(moe_w_up), bf(moe_w_down)

    rows8 = -(-(bt + 1) // 8) * 8
    cvec = jnp.zeros((rows8, d), F32).at[:bt].set(c).at[bt].set(c_ctx)
    mods = _adaln(cvec, w_ada, b_ada)

    tabs_x = _fft_tables(seq)
    tabs_c = _fft_tables(lc)
    x2 = x.reshape(bt * seq, d)
    c2 = ctx.reshape(bt * lc, d)
    for l in range(depth):
        last = l == depth - 1
        mod_x = mods[l, :bt].reshape(bt, N_MOD, d)
        mod_c = jnp.broadcast_to(mods[l, bt].reshape(1, N_MOD, d), (bt, N_MOD, d))
        bias = _bias_tables(na_rpb[l], seq // GRID_W)
        dense = l % 2 == 0
        j = l // 2
        if last:
            outs = _inproj(c2, mod_c, norm1_g[l], p["w_in"][l], tabs_c[3], lc)
            kv_ctx = (outs[3], outs[4])
        else:
            c_mid, kv_ctx = _token_mix(c2, mod_c, l, lc, bt, p, None, tabs_c, None)
            if not dense:
                raise NotImplementedError("MoE channel mixer on context tokens (only reached when depth > 2)")
            c2 = _ffn(c_mid, mod_c, norm2_g[l], ffn_g[j], ffn_u[j], ffn_d[j], final_g, lc, False)
        x_mid, _ = _token_mix(x2, mod_x, l, seq, bt, p, kv_ctx, tabs_x, bias)
        if dense:
            x2 = _ffn(x_mid, mod_x, norm2_g[l], ffn_g[j], ffn_u[j], ffn_d[j], final_g, seq, last)
        else:
            h_bf, route = _router(x_mid, mod_x, norm2_g[l], moe_router[j], seq)
            plan = _moe_plan(route, n_exp)
            hs = _moe_gather(h_bf, plan)
            y = _moe_experts(hs, moe_g[j], moe_u[j], moe_d[j], plan)
            x2 = _moe_combine(y, x_mid, mod_x, final_g, plan, seq, last)
    return x2.reshape(bt, seq, d)
```

```python
import functools

import numpy as np
import jax
import jax.numpy as jnp
from jax import lax
from jax.experimental import pallas as pl
from jax.experimental.pallas import tpu as pltpu

F32 = jnp.float32
BF16 = jnp.bfloat16
I32 = jnp.int32

EPS = 1e-6
N_MOD = 6
GRID_W = 64
WIN_H = 8
WIN_W = 16
F_GROUPS = 4
F_GROUP_DIM = 64
F_WIDTH = F_GROUPS * F_GROUP_DIM
NA_HEADS = 8
NA_HEAD_DIM = 64
NA_WIDTH = NA_HEADS * NA_HEAD_DIM
CONV_WIDTH = 256
CONV_K = 3
N_BRANCH = 3
TOP_K = 2
Q_OFF = F_WIDTH
K_OFF = Q_OFF + NA_WIDTH
V_OFF = K_OFF + NA_WIDTH
U_OFF = V_OFF + NA_WIDTH
B_OFF = U_OFF + CONV_WIDTH
C_OFF = B_OFF + CONV_WIDTH
G_OFF = C_OFF + CONV_WIDTH

LANES = 128
BF16_SUBLANES = 16
VMEM_LIMIT_BYTES = 56 * 1024 * 1024

ROW_TILE = 512
Q_ROWS = 8
KEY_ROWS = 16
MASK_VALUE = -1e30
MOE_EXPERT_TILE = 512
MOE_SLOT_TILE = 256
MOE_TOKEN_CHUNK = 512
MOE_GATHER_CHUNK = 1024
MOE_FF_CHUNK = 256


def _params(sem):
    return pltpu.CompilerParams(dimension_semantics=sem, vmem_limit_bytes=VMEM_LIMIT_BYTES)


def _const_spec(shape):
    nd = len(shape)
    return pl.BlockSpec(shape, lambda *_: (0,) * nd, pipeline_mode=pl.Buffered(1))


def _dot(a, b):
    return jnp.dot(a, b, preferred_element_type=F32)


def _split_bf16(x):
    hi = x.astype(BF16)
    lo = (x - hi.astype(F32)).astype(BF16)
    return hi, lo


def _dot_split(a, w):
    ah, al = _split_bf16(a)
    wh, wl = _split_bf16(w)
    return _dot(ah, wh) + (_dot(ah, wl) + _dot(al, wh))


def _norm_mod(x, g, shift, scale):
    ms = jnp.mean(x * x, axis=-1, keepdims=True)
    y = x * lax.rsqrt(ms + EPS) * g
    return y * (1.0 + scale) + shift


def _rmsnorm(x, g):
    ms = jnp.mean(x * x, axis=-1, keepdims=True)
    return x * lax.rsqrt(ms + EPS) * g


def _adaln_kernel(c_ref, w_ref, b_ref, o_ref):
    a = c_ref[...]
    a = a * jax.nn.sigmoid(a)
    o_ref[0] = _dot_split(a, w_ref[0]) + b_ref[0]


def _adaln(cvec, w_ada, b_ada):
    depth, d, width = w_ada.shape
    rows = cvec.shape[0]
    tn = width // 4
    return pl.pallas_call(
        _adaln_kernel,
        grid=(depth, width // tn),
        in_specs=[
            pl.BlockSpec((rows, d), lambda l, j: (0, 0)),
            pl.BlockSpec((1, d, tn), lambda l, j: (l, 0, j)),
            pl.BlockSpec((1, 1, tn), lambda l, j: (l, 0, j)),
        ],
        out_specs=pl.BlockSpec((1, rows, tn), lambda l, j: (l, 0, j)),
        out_shape=jax.ShapeDtypeStruct((depth, rows, width), F32),
        compiler_params=_params(("arbitrary", "arbitrary")),
        name="adaln",
    )(cvec, w_ada, b_ada.reshape(depth, 1, width))


def _inproj_kernel(x_ref, mod_ref, g_ref, w_ref, bd_ref,
                   a0_ref, b0_ref, q_ref, k_ref, v_ref, u_ref, b_ref, c_ref, gate_ref, *, d):
    h = _norm_mod(x_ref[...], g_ref[...], mod_ref[0, 0:1, :], mod_ref[0, 1:2, :]).astype(BF16)

    def proj(c0, c1):
        return _dot(h, w_ref[:, c0:c1])

    zf = proj(0, Q_OFF).astype(BF16)
    ab = _dot(zf, bd_ref[...])
    a0_ref[...] = ab[:, :F_WIDTH].astype(BF16)
    b0_ref[...] = ab[:, F_WIDTH:].astype(BF16)
    q_ref[...] = (proj(Q_OFF, K_OFF) * (NA_HEAD_DIM ** -0.5)).astype(BF16)
    k_ref[...] = proj(K_OFF, V_OFF).astype(BF16)
    v_ref[...] = proj(V_OFF, U_OFF).astype(BF16)
    u_ref[...] = proj(U_OFF, B_OFF).astype(BF16)
    b_ref[...] = proj(B_OFF, C_OFF).astype(BF16)
    c_ref[...] = proj(C_OFF, G_OFF).astype(BF16)
    step = 512 if d % 512 == 0 else d
    for j in range(0, N_BRANCH * d, step):
        gate_ref[:, j:j + step] = jax.nn.sigmoid(proj(G_OFF + j, G_OFF + j + step)).astype(BF16)


def _inproj(x2, mod, g, w_bf, bd, seq_len):
    n, d = x2.shape
    tm = min(ROW_TILE, seq_len)
    pw = w_bf.shape[1]
    widths = [F_WIDTH, F_WIDTH, NA_WIDTH, NA_WIDTH, NA_WIDTH, CONV_WIDTH, CONV_WIDTH, CONV_WIDTH, N_BRANCH * d]
    row = lambda w: pl.BlockSpec((tm, w), lambda i: (i, 0))
    return pl.pallas_call(
        functools.partial(_inproj_kernel, d=d),
        grid=(n // tm,),
        in_specs=[
            row(d),
            pl.BlockSpec((1, N_MOD, d), lambda i: (i * tm // seq_len, 0, 0)),
            _const_spec((1, d)),
            _const_spec((d, pw)),
            _const_spec((F_WIDTH, 2 * F_WIDTH)),
        ],
        out_specs=[row(w) for w in widths],
        out_shape=[jax.ShapeDtypeStruct((n, w), BF16) for w in widths],
        compiler_params=_params(("parallel",)),
        name="inproj",
    )(x2, mod, g.reshape(1, d), w_bf, bd)


def _fft_split(seq_len):
    b = 64 if seq_len >= 1024 else 16
    return seq_len // b, b


def _fft_tables(seq_len):
    a_pts, b_pts = _fft_split(seq_len)
    ia = np.arange(a_pts)
    ang = 2.0 * np.pi * ((ia[:, None] * ia[None, :]) % a_pts) / a_pts
    c, s = np.cos(ang) / np.sqrt(a_pts), np.sin(ang) / np.sqrt(a_pts)
    w1 = np.block([[c, -s], [-s, -c]])
    ib = np.arange(b_pts)
    num = (ib[None, :, None] * ib[None, None, :] * a_pts + ib[None, None, :] * ia[:, None, None]) % seq_len
    th = 2.0 * np.pi * num / seq_len
    mc, ms = np.cos(th) / np.sqrt(b_pts), np.sin(th) / np.sqrt(b_pts)
    ic = np.arange(F_GROUP_DIM)
    angc = 2.0 * np.pi * ((ic[:, None] * ic[None, :]) % F_GROUP_DIM) / F_GROUP_DIM
    eye = np.eye(F_GROUPS)
    bd = np.concatenate([np.kron(eye, np.cos(angc)), np.kron(eye, np.sin(angc))], axis=1) / np.sqrt(F_GROUP_DIM)
    return tuple(jnp.asarray(t, F32).astype(BF16) for t in (w1, mc, ms, bd))


def _fft1_kernel(a_ref, b_ref, w_ref, o_ref, *, a_pts):
    x = jnp.concatenate([a_ref[0], b_ref[0]], axis=0)
    t = _dot(w_ref[...], x)
    o_ref[0, 0] = t[:a_pts].astype(BF16)
    o_ref[0, 1] = t[a_pts:].astype(BF16)


def _fft2_kernel(tr_ref, ti_ref, mc_ref, ms_ref, o_ref, *, ac):
    for j in range(ac):
        y = _dot(mc_ref[j], tr_ref[0, 0, j]) + _dot(ms_ref[j], ti_ref[0, 0, j])
        o_ref[0, j] = y.astype(BF16)


def _fourier(a0, b0, bt, seq_len, w1, mc, ms):
    a_pts, b_pts = _fft_split(seq_len)
    c = F_WIDTH
    bc = b_pts * c
    cw = min(2048, bc)
    view = lambda z: z.reshape(bt, a_pts, bc)
    t = pl.pallas_call(
        functools.partial(_fft1_kernel, a_pts=a_pts),
        grid=(bt, bc // cw),
        in_specs=[
            pl.BlockSpec((1, a_pts, cw), lambda b, j: (b, 0, j)),
            pl.BlockSpec((1, a_pts, cw), lambda b, j: (b, 0, j)),
            _const_spec((2 * a_pts, 2 * a_pts)),
        ],
        out_specs=pl.BlockSpec((1, 2, a_pts, cw), lambda b, j: (b, 0, 0, j)),
        out_shape=jax.ShapeDtypeStruct((bt, 2, a_pts, bc), BF16),
        compiler_params=_params(("parallel", "parallel")),
        name="fft1",
    )(view(a0), view(b0), w1)
    t5 = t.reshape(bt, 2, a_pts, b_pts, c)
    ac = min(a_pts, 16)
    out = pl.pallas_call(
        functools.partial(_fft2_kernel, ac=ac),
        grid=(a_pts // ac, bt),
        in_specs=[
            pl.BlockSpec((1, 1, ac, b_pts, c), lambda i, b: (b, 0, i, 0, 0)),
            pl.BlockSpec((1, 1, ac, b_pts, c), lambda i, b: (b, 1, i, 0, 0)),
            pl.BlockSpec((ac, b_pts, b_pts), lambda i, b: (i, 0, 0)),
            pl.BlockSpec((ac, b_pts, b_pts), lambda i, b: (i, 0, 0)),
        ],
        out_specs=pl.BlockSpec((1, ac, b_pts, c), lambda i, b: (b, i, 0, 0)),
        out_shape=jax.ShapeDtypeStruct((bt, a_pts, b_pts, c), BF16),
        compiler_params=_params(("parallel", "parallel")),
        name="fft2",
    )(t5, t5, mc, ms)
    return out.reshape(bt * a_pts, bc)


def _window_rows(rows):
    r0s = np.array([0, Q_ROWS, rows - Q_ROWS])
    i = np.arange(Q_ROWS)
    j = np.arange(KEY_ROWS)
    r = r0s[:, None, None] + i[None, :, None]
    kr = r0s[:, None, None] - WIN_H // 2 + j[None, None, :]
    base = np.clip(r - WIN_H // 2, 0, rows - WIN_H)
    ok = (kr >= 0) & (kr < rows) & (kr >= base) & (kr < base + WIN_H)
    return tuple(tuple(tuple(bool(b) for b in row) for row in blk) for blk in ok)


def _biastab_kernel(rpb_ref, o_ref, *, ok_r):
    w = GRID_W
    cq = lax.broadcasted_iota(I32, (w, LANES), 0)
    kc = lax.broadcasted_iota(I32, (w, LANES), 1)
    cs = jnp.clip(cq - WIN_W // 2, 0, w - WIN_W)
    ok_c = (kc >= cs) & (kc < cs + WIN_W)
    low = kc < w
    masked = jnp.full((w, LANES), MASK_VALUE, F32)
    lo, hi = [], []
    for dr in range(2 * WIN_H - 1):
        xb = jnp.broadcast_to(rpb_ref[0, dr:dr + 1, :], (w, LANES))
        t = pltpu.roll(xb, LANES - (WIN_W - 1), 1, stride=1, stride_axis=0)
        t = jnp.where(ok_c, t, MASK_VALUE)
        lo.append(t)
        hi.append(pltpu.roll(t, w, 1))
    for v in range(3):
        for i in range(Q_ROWS):
            for jj in range(KEY_ROWS // 2):
                j0 = 2 * jj
                d0 = j0 - i + WIN_H // 2 - 1
                a = lo[d0] if ok_r[v][i][j0] else masked
                b = hi[d0 + 1] if ok_r[v][i][j0 + 1] else masked
                blk = masked if (a is masked and b is masked) else jnp.where(low, a, b)
                o_ref[v, 0, i * w:(i + 1) * w, jj * LANES:(jj + 1) * LANES] = blk.astype(BF16)


def _bias_tables(rpb, rows):
    n_dr, n_dc = rpb.shape[1], rpb.shape[2]
    rp = jnp.zeros((NA_HEADS, 2 * WIN_H, LANES), F32).at[:, :n_dr, :n_dc].set(rpb.astype(F32))
    qb, wk = Q_ROWS * GRID_W, KEY_ROWS * GRID_W
    return pl.pallas_call(
        functools.partial(_biastab_kernel, ok_r=_window_rows(rows)),
        grid=(NA_HEADS,),
        in_specs=[pl.BlockSpec((1, 2 * WIN_H, LANES), lambda h: (h, 0, 0))],
        out_specs=pl.BlockSpec((3, 1, qb, wk), lambda h: (0, h, 0, 0)),
        out_shape=jax.ShapeDtypeStruct((3, NA_HEADS, qb, wk), BF16),
        compiler_params=_params(("parallel",)),
        name="biastab",
    )(rp)


def _softmax_pv(s_parts, v_parts):
    m = s_parts[0].max(axis=-1, keepdims=True)
    for s in s_parts[1:]:
        m = jnp.maximum(m, s.max(axis=-1, keepdims=True))
    l = None
    pv = None
    for s, v in zip(s_parts, v_parts):
        p = jnp.exp(s - m)
        ls = p.sum(axis=-1, keepdims=True)
        o = _dot(p.astype(BF16), v)
        l = ls if l is None else l + ls
        pv = o if pv is None else pv + o
    return pv * (1.0 / l)


def _natten_kernel(q_ref, kp_ref, kc_ref, kn_ref, vp_ref, vc_ref, vn_ref, kx_ref, vx_ref, bias_ref,
                   o_ref, kw_ref, vw_ref, *, lc):
    qb = Q_ROWS * GRID_W
    halo = (KEY_ROWS - Q_ROWS) // 2 * GRID_W
    wk = KEY_ROWS * GRID_W
    for dst, (p, c, n, x) in ((kw_ref, (kp_ref, kc_ref, kn_ref, kx_ref)), (vw_ref, (vp_ref, vc_ref, vn_ref, vx_ref))):
        dst[0:halo] = p[qb - halo:qb]
        dst[halo:halo + qb] = c[...]
        dst[halo + qb:wk] = n[0:halo]
        dst[wk:wk + lc] = x[...]
    lane = lax.broadcasted_iota(I32, (qb, LANES), 1)
    low = lane < NA_HEAD_DIM
    for hp in range(NA_HEADS // 2):
        sl = slice(LANES * hp, LANES * (hp + 1))
        qp = q_ref[:, sl]
        kw = kw_ref[:, sl]
        vw = vw_ref[:, sl]
        outs = []
        for sub in range(2):
            qm = jnp.where(low if sub == 0 else jnp.logical_not(low), qp, jnp.zeros_like(qp))
            s = lax.dot_general(qm, kw, (((1,), (1,)), ((), ())), preferred_element_type=F32)
            s_win = s[:, :wk] + bias_ref[0, 2 * hp + sub].astype(F32)
            outs.append(_softmax_pv([s_win, s[:, wk:]], [vw[:wk], vw[wk:]]))
        o_ref[:, sl] = jnp.where(low, outs[0], outs[1]).astype(BF16)


def _natten(q, k, v, kx, vx, bias, bt, seq_len, lc):
    n = q.shape[0]
    qb = Q_ROWS * GRID_W
    nb = seq_len // qb
    assert seq_len % qb == 0 and nb >= 2, "neighbourhood attention kernel needs >= 16 grid rows, a multiple of 8"
    wk = KEY_ROWS * GRID_W
    cur = lambda b, r: (b * nb + r, 0)
    prev = lambda b, r: (b * nb + jnp.maximum(r - 1, 0), 0)
    nxt = lambda b, r: (b * nb + jnp.minimum(r + 1, nb - 1), 0)
    blk = lambda im: pl.BlockSpec((qb, NA_WIDTH), im)
    variant = lambda b, r: (jnp.where(r == 0, 0, jnp.where(r == nb - 1, 2, 1)), 0, 0, 0)
    return pl.pallas_call(
        functools.partial(_natten_kernel, lc=lc),
        grid=(bt, nb),
        in_specs=[
            blk(cur), blk(prev), blk(cur), blk(nxt), blk(prev), blk(cur), blk(nxt),
            pl.BlockSpec((lc, NA_WIDTH), lambda b, r: (b, 0)),
            pl.BlockSpec((lc, NA_WIDTH), lambda b, r: (b, 0)),
            pl.BlockSpec((1, NA_HEADS, qb, wk), variant),
        ],
        out_specs=blk(cur),
        out_shape=jax.ShapeDtypeStruct((n, NA_WIDTH), BF16),
        scratch_shapes=[pltpu.VMEM((wk + lc, NA_WIDTH), BF16), pltpu.VMEM((wk + lc, NA_WIDTH), BF16)],
        compiler_params=_params(("parallel", "arbitrary")),
        name="natten",
    )(q, k, k, k, v, v, v, kx, vx, bias)


def _ctxattn_kernel(q_ref, k_ref, v_ref, o_ref):
    lc = q_ref.shape[0]
    lane = lax.broadcasted_iota(I32, (lc, LANES), 1)
    low = lane < NA_HEAD_DIM
    for hp in range(NA_HEADS // 2):
        sl = slice(LANES * hp, LANES * (hp + 1))
        qp = q_ref[:, sl]
        kw = k_ref[:, sl]
        vw = v_ref[:, sl]
        outs = []
        for sub in range(2):
            qm = jnp.where(low if sub == 0 else jnp.logical_not(low), qp, jnp.zeros_like(qp))
            s = lax.dot_general(qm, kw, (((1,), (1,)), ((), ())), preferred_element_type=F32)
            outs.append(_softmax_pv([s], [vw]))
        o_ref[:, sl] = jnp.where(low, outs[0], outs[1]).astype(BF16)


def _ctxattn(q, k, v, bt, lc):
    blk = pl.BlockSpec((lc, NA_WIDTH), lambda b: (b, 0))
    return pl.pallas_call(
        _ctxattn_kernel,
        grid=(bt,),
        in_specs=[blk, blk, blk],
        out_specs=blk,
        out_shape=jax.ShapeDtypeStruct(q.shape, BF16),
        compiler_params=_params(("parallel",)),
        name="ctxattn",
    )(q, k, v)


def _merge_kernel(fm_ref, at_ref, u_ref, b_ref, c_ref, up_ref, cp_ref, un_ref, cn_ref, gate_ref, x_ref, mod_ref,
                  cw_ref, wf_ref, wa_ref, wc_ref, wo_ref, o_ref, pre_ref, *, tm, d, a_pts, seq_len):
    i = pl.program_id(0)
    t0 = i * tm
    first = (t0 % seq_len) == 0
    last = ((t0 + tm) % seq_len) == 0
    p = c_ref[...].astype(F32) * u_ref[...].astype(F32)
    hs = BF16_SUBLANES
    p_above = cp_ref[hs - 1:hs, :].astype(F32) * up_ref[hs - 1:hs, :].astype(F32)
    p_below = cn_ref[0:1, :].astype(F32) * un_ref[0:1, :].astype(F32)
    p_above = jnp.where(first, 0.0, p_above)
    p_below = jnp.where(last, 0.0, p_below)
    row = lax.broadcasted_iota(I32, (tm, CONV_WIDTH), 0)
    p_dn = jnp.where(row == 0, p_above, pltpu.roll(p, 1, 0))
    p_up = jnp.where(row == tm - 1, p_below, pltpu.roll(p, tm - 1, 0))
    cv = cw_ref[0:1, :] * p_dn + cw_ref[1:2, :] * p + cw_ref[2:3, :] * p_up
    y_c = _dot((b_ref[...].astype(F32) * cv).astype(BF16), wc_ref[...])
    y_a = _dot(at_ref[...], wa_ref[...])
    pre_ref[...] = (gate_ref[:, d:2 * d].astype(F32) * y_a + gate_ref[:, 2 * d:3 * d].astype(F32) * y_c)
    for j in range(tm // a_pts):
        rs = slice(j * a_pts, (j + 1) * a_pts)
        y_f = _dot(fm_ref[:, j * F_WIDTH:(j + 1) * F_WIDTH], wf_ref[...])
        pre_ref[rs, :] += gate_ref[rs, 0:d].astype(F32) * y_f
    mix = _dot(pre_ref[...].astype(BF16), wo_ref[...])
    o_ref[...] = x_ref[...] + mod_ref[0, 2:3, :] * mix


def _merge(fm, attn, zu, zb, zc, gates, x2, mod, conv_w, wf, wa, wc, wo, seq_len):
    n, d = x2.shape
    tm = min(ROW_TILE, seq_len)
    a_pts, b_pts = _fft_split(seq_len)
    assert tm % a_pts == 0
    hs = BF16_SUBLANES
    nh = n // hs
    per = tm // hs
    row = lambda w: pl.BlockSpec((tm, w), lambda i: (i, 0))
    above = pl.BlockSpec((hs, CONV_WIDTH), lambda i: (jnp.maximum(i * per - 1, 0), 0))
    below = pl.BlockSpec((hs, CONV_WIDTH), lambda i: (jnp.minimum((i + 1) * per, nh - 1), 0))
    tiles_per_seq = seq_len // tm
    sub = tm // a_pts
    fm_spec = pl.BlockSpec((a_pts, sub * F_WIDTH), lambda i: (i // tiles_per_seq, i % tiles_per_seq))
    return pl.pallas_call(
        functools.partial(_merge_kernel, tm=tm, d=d, a_pts=a_pts, seq_len=seq_len),
        grid=(n // tm,),
        in_specs=[
            fm_spec, row(NA_WIDTH), row(CONV_WIDTH), row(CONV_WIDTH), row(CONV_WIDTH),
            above, above, below, below,
            row(N_BRANCH * d), row(d),
            pl.BlockSpec((1, N_MOD, d), lambda i: (i * tm // seq_len, 0, 0)),
            _const_spec((CONV_K, CONV_WIDTH)),
            _const_spec((F_WIDTH, d)), _const_spec((NA_WIDTH, d)), _const_spec((CONV_WIDTH, d)), _const_spec((d, d)),
        ],
        out_specs=row(d),
        out_shape=jax.ShapeDtypeStruct((n, d), F32),
        scratch_shapes=[pltpu.VMEM((tm, d), F32)],
        compiler_params=_params(("parallel",)),
        name="merge",
    )(fm, attn, zu, zb, zc, zu, zc, zu, zc, gates, x2, mod, conv_w, wf, wa, wc, wo)


def _ffn_kernel(x_ref, mod_ref, g_ref, wg_ref, wu_ref, wd_ref, fg_ref, o_ref, acc_ref, *, tf, final):
    x = x_ref[...]
    h = _norm_mod(x, g_ref[...], mod_ref[0, 3:4, :], mod_ref[0, 4:5, :]).astype(BF16)
    ff = wg_ref.shape[1]
    for f in range(0, ff, tf):
        g = _dot(h, wg_ref[:, f:f + tf])
        u = _dot(h, wu_ref[:, f:f + tf])
        a = (g * jax.nn.sigmoid(g) * u).astype(BF16)
        y = _dot(a, wd_ref[f:f + tf, :])
        if f == 0:
            acc_ref[...] = y
        else:
            acc_ref[...] += y
    out = x + mod_ref[0, 5:6, :] * acc_ref[...]
    o_ref[...] = _rmsnorm(out, fg_ref[...]) if final else out


def _ffn(x2, mod, g, wg, wu, wd, final_g, seq_len, final):
    n, d = x2.shape
    ff = wg.shape[1]
    tm = min(ROW_TILE, seq_len)
    tf = 256 if ff % 256 == 0 else LANES
    row = pl.BlockSpec((tm, d), lambda i: (i, 0))
    return pl.pallas_call(
        functools.partial(_ffn_kernel, tf=tf, final=final),
        grid=(n // tm,),
        in_specs=[
            row,
            pl.BlockSpec((1, N_MOD, d), lambda i: (i * tm // seq_len, 0, 0)),
            _const_spec((1, d)), _const_spec((d, ff)), _const_spec((d, ff)), _const_spec((ff, d)),
            _const_spec((1, d)),
        ],
        out_specs=row,
        out_shape=jax.ShapeDtypeStruct((n, d), F32),
        scratch_shapes=[pltpu.VMEM((tm, d), F32)],
        compiler_params=_params(("parallel",)),
        name="ffn",
    )(x2, mod, g.reshape(1, d), wg, wu, wd, final_g.reshape(1, d))


def _router_kernel(x_ref, mod_ref, g_ref, wr_ref, h_ref, r_ref, *, n_exp):
    h = _norm_mod(x_ref[...], g_ref[...], mod_ref[0, 3:4, :], mod_ref[0, 4:5, :])
    h_ref[...] = h.astype(BF16)
    logits = _dot_split(h, wr_ref[...])
    lane = lax.broadcasted_iota(I32, logits.shape, 1)
    lane_f = lane.astype(F32)
    neg = -jnp.inf
    l1 = jnp.where(lane < n_exp, logits, neg)
    m1 = l1.max(axis=-1, keepdims=True)
    i1 = jnp.where(l1 == m1, lane_f, float(LANES)).min(axis=-1, keepdims=True)
    l2 = jnp.where(lane_f == i1, neg, l1)
    m2 = l2.max(axis=-1, keepdims=True)
    i2 = jnp.where(l2 == m2, lane_f, float(LANES)).min(axis=-1, keepdims=True)
    e = jnp.exp(m2 - m1)
    w1 = 1.0 / (1.0 + e)
    w2 = e * w1
    out = jnp.where(lane == 0, i1, 0.0)
    out = jnp.where(lane == 1, i2, out)
    out = jnp.where(lane == 2, w1, out)
    out = jnp.where(lane == 3, w2, out)
    r_ref[...] = out


def _router(x2, mod, g, w_router, seq_len):
    n, d = x2.shape
    n_exp = w_router.shape[1]
    tm = min(ROW_TILE, seq_len)
    wr = jnp.zeros((d, LANES), F32).at[:, :n_exp].set(w_router)
    row = lambda w: pl.BlockSpec((tm, w), lambda i: (i, 0))
    return pl.pallas_call(
        functools.partial(_router_kernel, n_exp=n_exp),
        grid=(n // tm,),
        in_specs=[
            row(d),
            pl.BlockSpec((1, N_MOD, d), lambda i: (i * tm // seq_len, 0, 0)),
            _const_spec((1, d)), _const_spec((d, LANES)),
        ],
        out_specs=[row(d), row(LANES)],
        out_shape=[jax.ShapeDtypeStruct((n, d), BF16), jax.ShapeDtypeStruct((n, LANES), F32)],
        compiler_params=_params(("parallel",)),
        name="router",
    )(x2, mod, g.reshape(1, d), wr)


def _count_le(sorted_vals, v):
    return jnp.sum((sorted_vals[None, :] <= v[:, None]).astype(I32), axis=1)


def _moe_plan(route, n_exp):
    n = route.shape[0]
    te, tg, tc = MOE_EXPERT_TILE, MOE_SLOT_TILE, MOE_TOKEN_CHUNK
    assert n % MOE_GATHER_CHUNK == 0 and MOE_GATHER_CHUNK % tc == 0 and te % tg == 0
    idx = route[:, :TOP_K].astype(I32)
    wts = route[:, TOP_K:2 * TOP_K]
    ex = jnp.arange(n_exp, dtype=I32)
    hit = idx[:, :, None] == ex[None, None, :]
    sel = hit.any(axis=1)
    gate = jnp.sum(jnp.where(hit, wts[:, :, None], 0.0), axis=1)
    seli = sel.astype(I32)
    cnt = seli.sum(axis=0)
    padded = (cnt + te - 1) // te * te
    ends = jnp.cumsum(padded)
    off = ends - padded
    total = ends[-1]
    rank = jnp.cumsum(seli, axis=0) - seli
    pos = jnp.where(sel, off[None, :] + rank, -1)
    n_slots = TOP_K * n + n_exp * te
    n_tg, n_te, n_ch = n_slots // tg, n_slots // te, n // tc
    g_max = n_exp * n_ch + n_tg

    posc = pos.reshape(n_ch, tc, n_exp)
    hasc = (posc >= 0).any(axis=1)
    pmin = jnp.where(posc >= 0, posc, n_slots).min(axis=1)
    pmax = posc.max(axis=1)

    te_base = jnp.arange(n_te, dtype=I32) * te
    te_valid = (te_base < total).astype(I32)
    last_valid = jnp.maximum(total // te - 1, 0)
    te_exp = jnp.minimum(_count_le(ends, jnp.minimum(te_base, last_valid * te)), n_exp - 1).astype(I32)

    tg_base = jnp.arange(n_tg, dtype=I32) * tg
    tg_exp = jnp.minimum(_count_le(ends, tg_base), n_exp - 1).astype(I32)
    used_end = off[tg_exp] + cnt[tg_exp]
    has = tg_base < used_end
    ratio = MOE_GATHER_CHUNK // tc
    n_gc = n_ch // ratio
    has_g = hasc.reshape(n_gc, ratio, n_exp).any(axis=1)
    pmin_g = pmin.reshape(n_gc, ratio, n_exp).min(axis=1)
    pmax_g = pmax.reshape(n_gc, ratio, n_exp).max(axis=1)
    ch = jnp.arange(n_gc, dtype=I32)[None, :]
    has_e = jnp.take(has_g.T, tg_exp, axis=0)
    first_c = jnp.where(has_e & (jnp.take(pmax_g.T, tg_exp, axis=0) >= tg_base[:, None]), ch, n_gc).min(axis=1)
    last_c = jnp.where(has_e & (jnp.take(pmin_g.T, tg_exp, axis=0) < tg_base[:, None] + tg), ch, -1).max(axis=1)
    c_lo = jnp.where(has, first_c, 0)
    c_hi = jnp.where(has, last_c, 0)
    n_it = c_hi - c_lo + 1
    start = jnp.cumsum(n_it) - n_it
    tot_g = start[-1] + n_it[-1]
    g = jnp.arange(g_max, dtype=I32)
    gc = jnp.minimum(g, tot_g - 1)
    s_of = (_count_le(start, gc) - 1).astype(I32)
    ga_valid = (g < tot_g).astype(I32)
    ga_tile = s_of
    ga_chunk = (c_lo[s_of] + gc - start[s_of]).astype(I32)
    ga_first = ((gc == start[s_of]) & (g < tot_g)).astype(I32)

    s_lo = jnp.where(hasc, pmin // tg, 0).reshape(-1)
    s_hi = jnp.where(hasc, pmax // tg, 0).reshape(-1)
    ccnt = jnp.where(hasc.reshape(-1), s_hi - s_lo + 1, 0)
    cend = jnp.cumsum(ccnt)
    cstart = cend - ccnt
    tot_c = cend[-1]
    hc = jnp.minimum(g, tot_c - 1)
    pair = (_count_le(cstart, hc) - 1).astype(I32)
    co_valid = (g < tot_c).astype(I32)
    co_chunk = (pair // n_exp).astype(I32)
    co_exp = (pair % n_exp).astype(I32)
    co_tile = (s_lo[pair] + hc - cstart[pair]).astype(I32)
    chunk_start = cstart.reshape(n_ch, n_exp)[:, 0]
    chunk_end = cend.reshape(n_ch, n_exp)[:, -1]
    co_first = ((hc == chunk_start[co_chunk]) & (g < tot_c)).astype(I32)
    co_last = ((hc == chunk_end[co_chunk] - 1) & (g < tot_c)).astype(I32)

    pos_t = pos.T.astype(I32)
    return dict(n_slots=n_slots, g_max=g_max, te_exp=te_exp, te_valid=te_valid, tg_exp=tg_exp,
                ga=(ga_tile, ga_chunk, ga_first, ga_valid), co=(co_tile, co_chunk, co_exp, co_first, co_last, co_valid),
                pos_t=pos_t, pos_f=pos.astype(F32), gate=gate)


def _gather_kernel(it_tile, it_chunk, it_first, it_valid, tg_exp, h_ref, pos_ref, o_ref):
    g = pl.program_id(0)
    tg, tc = MOE_SLOT_TILE, MOE_GATHER_CHUNK

    @pl.when(it_valid[g] == 1)
    def _():
        s = it_tile[g]
        e = tg_exp[s]
        pos_row = pos_ref[pl.ds(e, 1), :]
        slot = s * tg + lax.broadcasted_iota(I32, (tg, tc), 0)
        onehot = jnp.where(pos_row == slot, 1.0, 0.0).astype(BF16)
        r = _dot(onehot, h_ref[...]).astype(BF16)

        @pl.when(it_first[g] == 1)
        def _():
            o_ref[...] = r

        @pl.when(it_first[g] == 0)
        def _():
            o_ref[...] += r


def _moe_gather(h_bf, plan):
    n, d = h_bf.shape
    n_exp = plan["pos_t"].shape[0]
    tg, tc = MOE_SLOT_TILE, MOE_GATHER_CHUNK
    grid_spec = pltpu.PrefetchScalarGridSpec(
        num_scalar_prefetch=5,
        grid=(plan["g_max"],),
        in_specs=[
            pl.BlockSpec((tc, d), lambda g, t, c, f, v, te: (c[g], 0)),
            pl.BlockSpec((n_exp, tc), lambda g, t, c, f, v, te: (0, c[g])),
        ],
        out_specs=pl.BlockSpec((tg, d), lambda g, t, c, f, v, te: (t[g], 0)),
    )
    return pl.pallas_call(
        _gather_kernel,
        grid_spec=grid_spec,
        out_shape=jax.ShapeDtypeStruct((plan["n_slots"], d), BF16),
        compiler_params=_params(("arbitrary",)),
        name="moe_gather",
    )(*plan["ga"], plan["tg_exp"], h_bf, plan["pos_t"])


def _experts_kernel(te_exp, te_valid, h_ref, wg_ref, wu_ref, wd_ref, y_ref, acc_ref, *, sub):
    i = pl.program_id(0)
    f = pl.program_id(1)

    @pl.when(f == 0)
    def _():
        acc_ref[...] = jnp.zeros_like(acc_ref)

    @pl.when(te_valid[i] == 1)
    def _():
        h = h_ref[...]
        for c in range(0, wg_ref.shape[2], sub):
            g = _dot(h, wg_ref[0, :, c:c + sub])
            u = _dot(h, wu_ref[0, :, c:c + sub])
            a = (g * jax.nn.sigmoid(g) * u).astype(BF16)
            acc_ref[...] += _dot(a, wd_ref[0, c:c + sub, :])

    @pl.when(f == pl.num_programs(1) - 1)
    def _():
        y_ref[...] = acc_ref[...].astype(BF16)


def _moe_experts(hs, wg, wu, wd, plan):
    n_slots, d = hs.shape
    ff = wg.shape[2]
    te = MOE_EXPERT_TILE
    sub = MOE_FF_CHUNK if ff % (2 * MOE_FF_CHUNK) == 0 else LANES
    tf = ff // 2 if (ff // 2) % sub == 0 else ff
    grid_spec = pltpu.PrefetchScalarGridSpec(
        num_scalar_prefetch=2,
        grid=(n_slots // te, ff // tf),
        in_specs=[
            pl.BlockSpec((te, d), lambda i, f, e, v: (i, 0)),
            pl.BlockSpec((1, d, tf), lambda i, f, e, v: (e[i], 0, f * v[i])),
            pl.BlockSpec((1, d, tf), lambda i, f, e, v: (e[i], 0, f * v[i])),
            pl.BlockSpec((1, tf, d), lambda i, f, e, v: (e[i], f * v[i], 0)),
        ],
        out_specs=pl.BlockSpec((te, d), lambda i, f, e, v: (i, 0)),
        scratch_shapes=[pltpu.VMEM((te, d), F32)],
    )
    return pl.pallas_call(
        functools.partial(_experts_kernel, sub=sub),
        grid_spec=grid_spec,
        out_shape=jax.ShapeDtypeStruct((n_slots, d), BF16),
        compiler_params=_params(("arbitrary", "arbitrary")),
        name="moe_experts",
    )(plan["te_exp"], plan["te_valid"], hs, wg, wu, wd)


def _combine_kernel(it_tile, it_chunk, it_exp, it_first, it_last, it_valid,
                    y_ref, pos_ref, gate_ref, x_ref, mod_ref, fg_ref, o_ref, acc_ref, *, final):
    g = pl.program_id(0)
    tg, tc = MOE_SLOT_TILE, MOE_TOKEN_CHUNK

    @pl.when(it_first[g] == 1)
    def _():
        acc_ref[...] = jnp.zeros_like(acc_ref)

    @pl.when(it_valid[g] == 1)
    def _():
        e = it_exp[g]
        lane = lax.broadcasted_iota(I32, pos_ref.shape, 1)
        pos_col = jnp.sum(jnp.where(lane == e, pos_ref[...], 0.0), axis=-1, keepdims=True)
        w_col = jnp.sum(jnp.where(lane == e, gate_ref[...], 0.0), axis=-1, keepdims=True)
        slot = (it_tile[g] * tg + lax.broadcasted_iota(I32, (tc, tg), 1)).astype(F32)
        onehot = jnp.where(pos_col == slot, 1.0, 0.0).astype(BF16)
        acc_ref[...] += w_col * _dot(onehot, y_ref[...])

    @pl.when(it_last[g] == 1)
    def _():
        out = x_ref[...] + mod_ref[0, 5:6, :] * acc_ref[...]
        o_ref[...] = _rmsnorm(out, fg_ref[...]) if final else out


def _moe_combine(y, x2, mod, final_g, plan, seq_len, final):
    n, d = x2.shape
    n_exp = plan["pos_t"].shape[0]
    tg, tc = MOE_SLOT_TILE, MOE_TOKEN_CHUNK
    assert seq_len % tc == 0
    grid_spec = pltpu.PrefetchScalarGridSpec(
        num_scalar_prefetch=6,
        grid=(plan["g_max"],),
        in_specs=[
            pl.BlockSpec((tg, d), lambda g, t, c, *_: (t[g], 0)),
            pl.BlockSpec((tc, n_exp), lambda g, t, c, *_: (c[g], 0)),
            pl.BlockSpec((tc, n_exp), lambda g, t, c, *_: (c[g], 0)),
            pl.BlockSpec((tc, d), lambda g, t, c, *_: (c[g], 0)),
            pl.BlockSpec((1, N_MOD, d), lambda g, t, c, *_: (c[g] * tc // seq_len, 0, 0)),
            pl.BlockSpec((1, d), lambda g, *_: (0, 0)),
        ],
        out_specs=pl.BlockSpec((tc, d), lambda g, t, c, *_: (c[g], 0)),
        scratch_shapes=[pltpu.VMEM((tc, d), F32)],
    )
    return pl.pallas_call(
        functools.partial(_combine_kernel, final=final),
        grid_spec=grid_spec,
        out_shape=jax.ShapeDtypeStruct((n, d), F32),
        compiler_params=_params(("arbitrary",)),
        name="moe_combine",
    )(*plan["co"], y, plan["pos_f"], plan["gate"], x2, mod, final_g.reshape(1, d))


def _token_mix(x2, mod, l, seq_len, bt, p, kv_ctx, tabs, bias):
    a0, b0, q, k, v, zu, zb, zc, gates = _inproj(x2, mod, p["norm1_g"][l], p["w_in"][l], tabs[3], seq_len)
    fm = _fourier(a0, b0, bt, seq_len, *tabs[:3])
    if kv_ctx is None:
        attn = _ctxattn(q, k, v, bt, seq_len)
    else:
        attn = _natten(q, k, v, kv_ctx[0], kv_ctx[1], bias, bt, seq_len, kv_ctx[0].shape[0] // bt)
    out = _merge(fm, attn, zu, zb, zc, gates, x2, mod, p["conv_w"][l], p["w_fourier"][l], p["w_na"][l],
                 p["w_conv_out"][l], p["w_out"][l], seq_len)
    return out, (k, v)


def kernel(x, c, ctx, c_ctx, norm1_g, norm2_g, w_ada, b_ada, w_in, conv_w, na_rpb, w_fourier, w_na, w_conv_out, w_out, ffn_w_gate, ffn_w_up, ffn_w_down, moe_router, moe_w_gate, moe_w_up, moe_w_down, final_g):
    bt, seq, d = x.shape
    lc = ctx.shape[1]
    depth = w_in.shape[0]
    n_exp = moe_router.shape[-1]
    bf = lambda w: w.astype(BF16)
    p = dict(norm1_g=norm1_g, w_in=bf(w_in), conv_w=conv_w, w_fourier=bf(w_fourier), w_na=bf(w_na),
             w_conv_out=bf(w_conv_out), w_out=bf(w_out))
    ffn_g, ffn_u, ffn_d = bf(ffn_w_gate), bf(ffn_w_up), bf(ffn_w_down)
    moe_g, moe_u, moe_d = bf(moe_w_gate), bf(moe_w_up), bf(moe_w_down)

    rows8 = -(-(bt + 1) // 8) * 8
    cvec = jnp.zeros((rows8, d), F32).at[:bt].set(c).at[bt].set(c_ctx)
    mods = _adaln(cvec, w_ada, b_ada)

    tabs_x = _fft_tables(seq)
    tabs_c = _fft_tables(lc)
    x2 = x.reshape(bt * seq, d)
    c2 = ctx.reshape(bt * lc, d)
    for l in range(depth):
        last = l == depth - 1
        mod_x = mods[l, :bt].reshape(bt, N_MOD, d)
        mod_c = jnp.broadcast_to(mods[l, bt].reshape(1, N_MOD, d), (bt, N_MOD, d))
        bias = _bias_tables(na_rpb[l], seq // GRID_W)
        dense = l % 2 == 0
        j = l // 2
        if last:
            outs = _inproj(c2, mod_c, norm1_g[l], p["w_in"][l], tabs_c[3], lc)
            kv_ctx = (outs[3], outs[4])
        else:
            c_mid, kv_ctx = _token_mix(c2, mod_c, l, lc, bt, p, None, tabs_c, None)
            if not dense:
                raise NotImplementedError("MoE channel mixer on context tokens (only reached when depth > 2)")
            c2 = _ffn(c_mid, mod_c, norm2_g[l], ffn_g[j], ffn_u[j], ffn_d[j], final_g, lc, False)
        x_mid, _ = _token_mix(x2, mod_x, l, seq, bt, p, kv_ctx, tabs_x, bias)
        if dense:
            x2 = _ffn(x_mid, mod_x, norm2_g[l], ffn_g[j], ffn_u[j], ffn_d[j], final_g, seq, last)
        else:
            h_bf, route = _router(x_mid, mod_x, norm2_g[l], moe_router[j], seq)
            plan = _moe_plan(route, n_exp)
            hs = _moe_gather(h_bf, plan)
            y = _moe_experts(hs, moe_g[j], moe_u[j], moe_d[j], plan)
            x2 = _moe_combine(y, x_mid, mod_x, final_g, plan, seq, last)
    return x2.reshape(bt, seq, d)
```

```python
import functools

import numpy as np
import jax
import jax.numpy as jnp
from jax import lax
from jax.experimental import pallas as pl
from jax.experimental.pallas import tpu as pltpu

F32 = jnp.float32
BF16 = jnp.bfloat16
I32 = jnp.int32

EPS = 1e-6
N_MOD = 6
GRID_W = 64
WIN_H = 8
WIN_W = 16
F_GROUPS = 4
F_GROUP_DIM = 64
F_WIDTH = F_GROUPS * F_GROUP_DIM
NA_HEADS = 8
NA_HEAD_DIM = 64
NA_WIDTH = NA_HEADS * NA_HEAD_DIM
CONV_WIDTH = 256
CONV_K = 3
N_BRANCH = 3
TOP_K = 2
Q_OFF = F_WIDTH
K_OFF = Q_OFF + NA_WIDTH
V_OFF = K_OFF + NA_WIDTH
U_OFF = V_OFF + NA_WIDTH
B_OFF = U_OFF + CONV_WIDTH
C_OFF = B_OFF + CONV_WIDTH
G_OFF = C_OFF + CONV_WIDTH

LANES = 128
BF16_SUBLANES = 16
VMEM_LIMIT_BYTES = 56 * 1024 * 1024

ROW_TILE = 512
Q_ROWS = 8
KEY_ROWS = 16
MASK_VALUE = -1e30
MOE_EXPERT_TILE = 512
MOE_SLOT_TILE = 256
MOE_TOKEN_CHUNK = 512
MOE_GATHER_CHUNK = 1024
MOE_FF_CHUNK = 256


def _params(sem):
    return pltpu.CompilerParams(dimension_semantics=sem, vmem_limit_bytes=VMEM_LIMIT_BYTES)


def _const_spec(shape):
    nd = len(shape)
    return pl.BlockSpec(shape, lambda *_: (0,) * nd, pipeline_mode=pl.Buffered(1))


def _dot(a, b):
    return jnp.dot(a, b, preferred_element_type=F32)


def _split_bf16(x):
    hi = x.astype(BF16)
    lo = (x - hi.astype(F32)).astype(BF16)
    return hi, lo


def _dot_split(a, w):
    ah, al = _split_bf16(a)
    wh, wl = _split_bf16(w)
    return _dot(ah, wh) + (_dot(ah, wl) + _dot(al, wh))


def _norm_mod(x, g, shift, scale):
    ms = jnp.mean(x * x, axis=-1, keepdims=True)
    y = x * lax.rsqrt(ms + EPS) * g
    return y * (1.0 + scale) + shift


def _rmsnorm(x, g):
    ms = jnp.mean(x * x, axis=-1, keepdims=True)
    return x * lax.rsqrt(ms + EPS) * g


def _adaln_kernel(c_ref, w_ref, b_ref, o_ref):
    a = c_ref[...]
    a = a * jax.nn.sigmoid(a)
    o_ref[0] = _dot_split(a, w_ref[0]) + b_ref[0]


def _adaln(cvec, w_ada, b_ada):
    depth, d, width = w_ada.shape
    rows = cvec.shape[0]
    tn = width // 4
    return pl.pallas_call(
        _adaln_kernel,
        grid=(depth, width // tn),
        in_specs=[
            pl.BlockSpec((rows, d), lambda l, j: (0, 0)),
            pl.BlockSpec((1, d, tn), lambda l, j: (l, 0, j)),
            pl.BlockSpec((1, 1, tn), lambda l, j: (l, 0, j)),
        ],
        out_specs=pl.BlockSpec((1, rows, tn), lambda l, j: (l, 0, j)),
        out_shape=jax.ShapeDtypeStruct((depth, rows, width), F32),
        compiler_params=_params(("arbitrary", "arbitrary")),
        name="adaln",
    )(cvec, w_ada, b_ada.reshape(depth, 1, width))


def _inproj_kernel(x_ref, mod_ref, g_ref, w_ref, bd_ref,
                   a0_ref, b0_ref, q_ref, k_ref, v_ref, u_ref, b_ref, c_ref, gate_ref, *, d):
    h = _norm_mod(x_ref[...], g_ref[...], mod_ref[0, 0:1, :], mod_ref[0, 1:2, :]).astype(BF16)

    def proj(c0, c1):
        return _dot(h, w_ref[:, c0:c1])

    zf = proj(0, Q_OFF).astype(BF16)
    ab = _dot(zf, bd_ref[...])
    a0_ref[...] = ab[:, :F_WIDTH].astype(BF16)
    b0_ref[...] = ab[:, F_WIDTH:].astype(BF16)
    q_ref[...] = (proj(Q_OFF, K_OFF) * (NA_HEAD_DIM ** -0.5)).astype(BF16)
    k_ref[...] = proj(K_OFF, V_OFF).astype(BF16)
    v_ref[...] = proj(V_OFF, U_OFF).astype(BF16)
    u_ref[...] = proj(U_OFF, B_OFF).astype(BF16)
    b_ref[...] = proj(B_OFF, C_OFF).astype(BF16)
    c_ref[...] = proj(C_OFF, G_OFF).astype(BF16)
    step = 512 if d % 512 == 0 else d
    for j in range(0, N_BRANCH * d, step):
        gate_ref[:, j:j + step] = jax.nn.sigmoid(proj(G_OFF + j, G_OFF + j + step)).astype(BF16)


def _inproj(x2, mod, g, w_bf, bd, seq_len):
    n, d = x2.shape
    tm = min(ROW_TILE, seq_len)
    pw = w_bf.shape[1]
    widths = [F_WIDTH, F_WIDTH, NA_WIDTH, NA_WIDTH, NA_WIDTH, CONV_WIDTH, CONV_WIDTH, CONV_WIDTH, N_BRANCH * d]
    row = lambda w: pl.BlockSpec((tm, w), lambda i: (i, 0))
    return pl.pallas_call(
        functools.partial(_inproj_kernel, d=d),
        grid=(n // tm,),
        in_specs=[
            row(d),
            pl.BlockSpec((1, N_MOD, d), lambda i: (i * tm // seq_len, 0, 0)),
            _const_spec((1, d)),
            _const_spec((d, pw)),
            _const_spec((F_WIDTH, 2 * F_WIDTH)),
        ],
        out_specs=[row(w) for w in widths],
        out_shape=[jax.ShapeDtypeStruct((n, w), BF16) for w in widths],
        compiler_params=_params(("parallel",)),
        name="inproj",
    )(x2, mod, g.reshape(1, d), w_bf, bd)


def _fft_split(seq_len):
    b = 64 if seq_len >= 1024 else 16
    return seq_len // b, b


def _fft_tables(seq_len):
    a_pts, b_pts = _fft_split(seq_len)
    ia = np.arange(a_pts)
    ang = 2.0 * np.pi * ((ia[:, None] * ia[None, :]) % a_pts) / a_pts
    c, s = np.cos(ang) / np.sqrt(a_pts), np.sin(ang) / np.sqrt(a_pts)
    w1 = np.block([[c, -s], [-s, -c]])
    ib = np.arange(b_pts)
    num = (ib[None, :, None] * ib[None, None, :] * a_pts + ib[None, None, :] * ia[:, None, None]) % seq_len
    th = 2.0 * np.pi * num / seq_len
    mc, ms = np.cos(th) / np.sqrt(b_pts), np.sin(th) / np.sqrt(b_pts)
    ic = np.arange(F_GROUP_DIM)
    angc = 2.0 * np.pi * ((ic[:, None] * ic[None, :]) % F_GROUP_DIM) / F_GROUP_DIM
    eye = np.eye(F_GROUPS)
    bd = np.concatenate([np.kron(eye, np.cos(angc)), np.kron(eye, np.sin(angc))], axis=1) / np.sqrt(F_GROUP_DIM)
    return tuple(jnp.asarray(t, F32).astype(BF16) for t in (w1, mc, ms, bd))


def _fft1_kernel(a_ref, b_ref, w_ref, o_ref, *, a_pts):
    x = jnp.concatenate([a_ref[0], b_ref[0]], axis=0)
    t = _dot(w_ref[...], x)
    o_ref[0, 0] = t[:a_pts].astype(BF16)
    o_ref[0, 1] = t[a_pts:].astype(BF16)


def _fft2_kernel(tr_ref, ti_ref, mc_ref, ms_ref, o_ref, *, ac):
    for j in range(ac):
        y = _dot(mc_ref[j], tr_ref[0, 0, j]) + _dot(ms_ref[j], ti_ref[0, 0, j])
        o_ref[0, j] = y.astype(BF16)


def _fourier(a0, b0, bt, seq_len, w1, mc, ms):
    a_pts, b_pts = _fft_split(seq_len)
    c = F_WIDTH
    bc = b_pts * c
    cw = min(2048, bc)
    view = lambda z: z.reshape(bt, a_pts, bc)
    t = pl.pallas_call(
        functools.partial(_fft1_kernel, a_pts=a_pts),
        grid=(bt, bc // cw),
        in_specs=[
            pl.BlockSpec((1, a_pts, cw), lambda b, j: (b, 0, j)),
            pl.BlockSpec((1, a_pts, cw), lambda b, j: (b, 0, j)),
            _const_spec((2 * a_pts, 2 * a_pts)),
        ],
        out_specs=pl.BlockSpec((1, 2, a_pts, cw), lambda b, j: (b, 0, 0, j)),
        out_shape=jax.ShapeDtypeStruct((bt, 2, a_pts, bc), BF16),
        compiler_params=_params(("parallel", "parallel")),
        name="fft1",
    )(view(a0), view(b0), w1)
    t5 = t.reshape(bt, 2, a_pts, b_pts, c)
    ac = min(a_pts, 16)
    out = pl.pallas_call(
        functools.partial(_fft2_kernel, ac=ac),
        grid=(a_pts // ac, bt),
        in_specs=[
            pl.BlockSpec((1, 1, ac, b_pts, c), lambda i, b: (b, 0, i, 0, 0)),
            pl.BlockSpec((1, 1, ac, b_pts, c), lambda i, b: (b, 1, i, 0, 0)),
            pl.BlockSpec((ac, b_pts, b_pts), lambda i, b: (i, 0, 0)),
            pl.BlockSpec((ac, b_pts, b_pts), lambda i, b: (i, 0, 0)),
        ],
        out_specs=pl.BlockSpec((1, ac, b_pts, c), lambda i, b: (b, i, 0, 0)),
        out_shape=jax.ShapeDtypeStruct((bt, a_pts, b_pts, c), BF16),
        compiler_params=_params(("parallel", "parallel")),
        name="fft2",
    )(t5, t5, mc, ms)
    return out.reshape(bt * a_pts, bc)


def _window_rows(rows):
    r0s = np.array([0, Q_ROWS, rows - Q_ROWS])
    i = np.arange(Q_ROWS)
    j = np.arange(KEY_ROWS)
    r = r0s[:, None, None] + i[None, :, None]
    kr = r0s[:, None, None] - WIN_H // 2 + j[None, None, :]
    base = np.clip(r - WIN_H // 2, 0, rows - WIN_H)
    ok = (kr >= 0) & (kr < rows) & (kr >= base) & (kr < base + WIN_H)
    return tuple(tuple(tuple(bool(b) for b in row) for row in blk) for blk in ok)


def _biastab_kernel(rpb_ref, o_ref, *, ok_r):
    w = GRID_W
    cq = lax.broadcasted_iota(I32, (w, LANES), 0)
    kc = lax.broadcasted_iota(I32, (w, LANES), 1)
    cs = jnp.clip(cq - WIN_W // 2, 0, w - WIN_W)
    ok_c = (kc >= cs) & (kc < cs + WIN_W)
    low = kc < w
    masked = jnp.full((w, LANES), MASK_VALUE, F32)
    lo, hi = [], []
    for dr in range(2 * WIN_H - 1):
        xb = jnp.broadcast_to(rpb_ref[0, dr:dr + 1, :], (w, LANES))
        t = pltpu.roll(xb, LANES - (WIN_W - 1), 1, stride=1, stride_axis=0)
        t = jnp.where(ok_c, t, MASK_VALUE)
        lo.append(t)
        hi.append(pltpu.roll(t, w, 1))
    for v in range(3):
        for i in range(Q_ROWS):
            for jj in range(KEY_ROWS // 2):
                j0 = 2 * jj
                d0 = j0 - i + WIN_H // 2 - 1
                a = lo[d0] if ok_r[v][i][j0] else masked
                b = hi[d0 + 1] if ok_r[v][i][j0 + 1] else masked
                blk = masked if (a is masked and b is masked) else jnp.where(low, a, b)
                o_ref[v, 0, i * w:(i + 1) * w, jj * LANES:(jj + 1) * LANES] = blk.astype(BF16)


def _bias_tables(rpb, rows):
    n_dr, n_dc = rpb.shape[1], rpb.shape[2]
    rp = jnp.zeros((NA_HEADS, 2 * WIN_H, LANES), F32).at[:, :n_dr, :n_dc].set(rpb.astype(F32))
    qb, wk = Q_ROWS * GRID_W, KEY_ROWS * GRID_W
    return pl.pallas_call(
        functools.partial(_biastab_kernel, ok_r=_window_rows(rows)),
        grid=(NA_HEADS,),
        in_specs=[pl.BlockSpec((1, 2 * WIN_H, LANES), lambda h: (h, 0, 0))],
        out_specs=pl.BlockSpec((3, 1, qb, wk), lambda h: (0, h, 0, 0)),
        out_shape=jax.ShapeDtypeStruct((3, NA_HEADS, qb, wk), BF16),
        compiler_params=_params(("parallel",)),
        name="biastab",
    )(rp)


def _softmax_pv(s_parts, v_parts):
    m = s_parts[0].max(axis=-1, keepdims=True)
    for s in s_parts[1:]:
        m = jnp.maximum(m, s.max(axis=-1, keepdims=True))
    l = None
    pv = None
    for s, v in zip(s_parts, v_parts):
        p = jnp.exp(s - m)
        ls = p.sum(axis=-1, keepdims=True)
        o = _dot(p.astype(BF16), v)
        l = ls if l is None else l + ls
        pv = o if pv is None else pv + o
    return pv * (1.0 / l)


def _natten_kernel(q_ref, kp_ref, kc_ref, kn_ref, vp_ref, vc_ref, vn_ref, kx_ref, vx_ref, bias_ref,
                   o_ref, kw_ref, vw_ref, *, lc):
    qb = Q_ROWS * GRID_W
    halo = (KEY_ROWS - Q_ROWS) // 2 * GRID_W
    wk = KEY_ROWS * GRID_W
    for dst, (p, c, n, x) in ((kw_ref, (kp_ref, kc_ref, kn_ref, kx_ref)), (vw_ref, (vp_ref, vc_ref, vn_ref, vx_ref))):
        dst[0:halo] = p[qb - halo:qb]
        dst[halo:halo + qb] = c[...]
        dst[halo + qb:wk] = n[0:halo]
        dst[wk:wk + lc] = x[...]
    lane = lax.broadcasted_iota(I32, (qb, LANES), 1)
    low = lane < NA_HEAD_DIM
    for hp in range(NA_HEADS // 2):
        sl = slice(LANES * hp, LANES * (hp + 1))
        qp = q_ref[:, sl]
        kw = kw_ref[:, sl]
        vw = vw_ref[:, sl]
        outs = []
        for sub in range(2):
            qm = jnp.where(low if sub == 0 else jnp.logical_not(low), qp, jnp.zeros_like(qp))
            s = lax.dot_general(qm, kw, (((1,), (1,)), ((), ())), preferred_element_type=F32)
            s_win = s[:, :wk] + bias_ref[0, 2 * hp + sub].astype(F32)
            outs.append(_softmax_pv([s_win, s[:, wk:]], [vw[:wk], vw[wk:]]))
        o_ref[:, sl] = jnp.where(low, outs[0], outs[1]).astype(BF16)


def _natten(q, k, v, kx, vx, bias, bt, seq_len, lc):
    n = q.shape[0]
    qb = Q_ROWS * GRID_W
    nb = seq_len // qb
    assert seq_len % qb == 0 and nb >= 2, "neighbourhood attention kernel needs >= 16 grid rows, a multiple of 8"
    wk = KEY_ROWS * GRID_W
    cur = lambda b, r: (b * nb + r, 0)
    prev = lambda b, r: (b * nb + jnp.maximum(r - 1, 0), 0)
    nxt = lambda b, r: (b * nb + jnp.minimum(r + 1, nb - 1), 0)
    blk = lambda im: pl.BlockSpec((qb, NA_WIDTH), im)
    variant = lambda b, r: (jnp.where(r == 0, 0, jnp.where(r == nb - 1, 2, 1)), 0, 0, 0)
    return pl.pallas_call(
        functools.partial(_natten_kernel, lc=lc),
        grid=(bt, nb),
        in_specs=[
            blk(cur), blk(prev), blk(cur), blk(nxt), blk(prev), blk(cur), blk(nxt),
            pl.BlockSpec((lc, NA_WIDTH), lambda b, r: (b, 0)),
            pl.BlockSpec((lc, NA_WIDTH), lambda b, r: (b, 0)),
            pl.BlockSpec((1, NA_HEADS, qb, wk), variant),
        ],
        out_specs=blk(cur),
        out_shape=jax.ShapeDtypeStruct((n, NA_WIDTH), BF16),
        scratch_shapes=[pltpu.VMEM((wk + lc, NA_WIDTH), BF16), pltpu.VMEM((wk + lc, NA_WIDTH), BF16)],
        compiler_params=_params(("parallel", "arbitrary")),
        name="natten",
    )(q, k, k, k, v, v, v, kx, vx, bias)


def _ctxattn_kernel(q_ref, k_ref, v_ref, o_ref):
    lc = q_ref.shape[0]
    lane = lax.broadcasted_iota(I32, (lc, LANES), 1)
    low = lane < NA_HEAD_DIM
    for hp in range(NA_HEADS // 2):
        sl = slice(LANES * hp, LANES * (hp + 1))
        qp = q_ref[:, sl]
        kw = k_ref[:, sl]
        vw = v_ref[:, sl]
        outs = []
        for sub in range(2):
            qm = jnp.where(low if sub == 0 else jnp.logical_not(low), qp, jnp.zeros_like(qp))
            s = lax.dot_general(qm, kw, (((1,), (1,)), ((), ())), preferred_element_type=F32)
            outs.append(_softmax_pv([s], [vw]))
        o_ref[:, sl] = jnp.where(low, outs[0], outs[1]).astype(BF16)


def _ctxattn(q, k, v, bt, lc):
    blk = pl.BlockSpec((lc, NA_WIDTH), lambda b: (b, 0))
    return pl.pallas_call(
        _ctxattn_kernel,
        grid=(bt,),
        in_specs=[blk, blk, blk],
        out_specs=blk,
        out_shape=jax.ShapeDtypeStruct(q.shape, BF16),
        compiler_params=_params(("parallel",)),
        name="ctxattn",
    )(q, k, v)


def _merge_kernel(fm_ref, at_ref, u_ref, b_ref, c_ref, up_ref, cp_ref, un_ref, cn_ref, gate_ref, x_ref, mod_ref,
                  cw_ref, wf_ref, wa_ref, wc_ref, wo_ref, o_ref, pre_ref, *, tm, d, a_pts, seq_len):
    i = pl.program_id(0)
    t0 = i * tm
    first = (t0 % seq_len) == 0
    last = ((t0 + tm) % seq_len) == 0
    p = c_ref[...].astype(F32) * u_ref[...].astype(F32)
    hs = BF16_SUBLANES
    p_above = cp_ref[hs - 1:hs, :].astype(F32) * up_ref[hs - 1:hs, :].astype(F32)
    p_below = cn_ref[0:1, :].astype(F32) * un_ref[0:1, :].astype(F32)
    p_above = jnp.where(first, 0.0, p_above)
    p_below = jnp.where(last, 0.0, p_below)
    row = lax.broadcasted_iota(I32, (tm, CONV_WIDTH), 0)
    p_dn = jnp.where(row == 0, p_above, pltpu.roll(p, 1, 0))
    p_up = jnp.where(row == tm - 1, p_below, pltpu.roll(p, tm - 1, 0))
    cv = cw_ref[0:1, :] * p_dn + cw_ref[1:2, :] * p + cw_ref[2:3, :] * p_up
    y_c = _dot((b_ref[...].astype(F32) * cv).astype(BF16), wc_ref[...])
    y_a = _dot(at_ref[...], wa_ref[...])
    pre_ref[...] = (gate_ref[:, d:2 * d].astype(F32) * y_a + gate_ref[:, 2 * d:3 * d].astype(F32) * y_c)
    for j in range(tm // a_pts):
        rs = slice(j * a_pts, (j + 1) * a_pts)
        y_f = _dot(fm_ref[:, j * F_WIDTH:(j + 1) * F_WIDTH], wf_ref[...])
        pre_ref[rs, :] += gate_ref[rs, 0:d].astype(F32) * y_f
    mix = _dot(pre_ref[...].astype(BF16), wo_ref[...])
    o_ref[...] = x_ref[...] + mod_ref[0, 2:3, :] * mix


def _merge(fm, attn, zu, zb, zc, gates, x2, mod, conv_w, wf, wa, wc, wo, seq_len):
    n, d = x2.shape
    tm = min(ROW_TILE, seq_len)
    a_pts, b_pts = _fft_split(seq_len)
    assert tm % a_pts == 0
    hs = BF16_SUBLANES
    nh = n // hs
    per = tm // hs
    row = lambda w: pl.BlockSpec((tm, w), lambda i: (i, 0))
    above = pl.BlockSpec((hs, CONV_WIDTH), lambda i: (jnp.maximum(i * per - 1, 0), 0))
    below = pl.BlockSpec((hs, CONV_WIDTH), lambda i: (jnp.minimum((i + 1) * per, nh - 1), 0))
    tiles_per_seq = seq_len // tm
    sub = tm // a_pts
    fm_spec = pl.BlockSpec((a_pts, sub * F_WIDTH), lambda i: (i // tiles_per_seq, i % tiles_per_seq))
    return pl.pallas_call(
        functools.partial(_merge_kernel, tm=tm, d=d, a_pts=a_pts, seq_len=seq_len),
        grid=(n // tm,),
        in_specs=[
            fm_spec, row(NA_WIDTH), row(CONV_WIDTH), row(CONV_WIDTH), row(CONV_WIDTH),
            above, above, below, below,
            row(N_BRANCH * d), row(d),
            pl.BlockSpec((1, N_MOD, d), lambda i: (i * tm // seq_len, 0, 0)),
            _const_spec((CONV_K, CONV_WIDTH)),
            _const_spec((F_WIDTH, d)), _const_spec((NA_WIDTH, d)), _const_spec((CONV_WIDTH, d)), _const_spec((d, d)),
        ],
        out_specs=row(d),
        out_shape=jax.ShapeDtypeStruct((n, d), F32),
        scratch_shapes=[pltpu.VMEM((tm, d), F32)],
        compiler_params=_params(("parallel",)),
        name="merge",
    )(fm, attn, zu, zb, zc, zu, zc, zu, zc, gates, x2, mod, conv_w, wf, wa, wc, wo)


def _ffn_kernel(x_ref, mod_ref, g_ref, wg_ref, wu_ref, wd_ref, fg_ref, o_ref, acc_ref, *, tf, final):
    x = x_ref[...]
    h = _norm_mod(x, g_ref[...], mod_ref[0, 3:4, :], mod_ref[0, 4:5, :]).astype(BF16)
    ff = wg_ref.shape[1]
    for f in range(0, ff, tf):
        g = _dot(h, wg_ref[:, f:f + tf])
        u = _dot(h, wu_ref[:, f:f + tf])
        a = (g * jax.nn.sigmoid(g) * u).astype(BF16)
        y = _dot(a, wd_ref[f:f + tf, :])
        if f == 0:
            acc_ref[...] = y
        else:
            acc_ref[...] += y
    out = x + mod_ref[0, 5:6, :] * acc_ref[...]
    o_ref[...] = _rmsnorm(out, fg_ref[...]) if final else out


def _ffn(x2, mod, g, wg, wu, wd, final_g, seq_len, final):
    n, d = x2.shape
    ff = wg.shape[1]
    tm = min(ROW_TILE, seq_len)
    tf = 256 if ff % 256 == 0 else LANES
    row = pl.BlockSpec((tm, d), lambda i: (i, 0))
    return pl.pallas_call(
        functools.partial(_ffn_kernel, tf=tf, final=final),
        grid=(n // tm,),
        in_specs=[
            row,
            pl.BlockSpec((1, N_MOD, d), lambda i: (i * tm // seq_len, 0, 0)),
            _const_spec((1, d)), _const_spec((d, ff)), _const_spec((d, ff)), _const_spec((ff, d)),
            _const_spec((1, d)),
        ],
        out_specs=row,
        out_shape=jax.ShapeDtypeStruct((n, d), F32),
        scratch_shapes=[pltpu.VMEM((tm, d), F32)],
        compiler_params=_params(("parallel",)),
        name="ffn",
    )(x2, mod, g.reshape(1, d), wg, wu, wd, final_g.reshape(1, d))


def _router_kernel(x_ref, mod_ref, g_ref, wr_ref, h_ref, r_ref, *, n_exp):
    h = _norm_mod(x_ref[...], g_ref[...], mod_ref[0, 3:4, :], mod_ref[0, 4:5, :])
    h_ref[...] = h.astype(BF16)
    logits = _dot_split(h, wr_ref[...])
    lane = lax.broadcasted_iota(I32, logits.shape, 1)
    lane_f = lane.astype(F32)
    neg = -jnp.inf
    l1 = jnp.where(lane < n_exp, logits, neg)
    m1 = l1.max(axis=-1, keepdims=True)
    i1 = jnp.where(l1 == m1, lane_f, float(LANES)).min(axis=-1, keepdims=True)
    l2 = jnp.where(lane_f == i1, neg, l1)
    m2 = l2.max(axis=-1, keepdims=True)
    i2 = jnp.where(l2 == m2, lane_f, float(LANES)).min(axis=-1, keepdims=True)
    e = jnp.exp(m2 - m1)
    w1 = 1.0 / (1.0 + e)
    w2 = e * w1
    out = jnp.where(lane == 0, i1, 0.0)
    out = jnp.where(lane == 1, i2, out)
    out = jnp.where(lane == 2, w1, out)
    out = jnp.where(lane == 3, w2, out)
    r_ref[...] = out


def _router(x2, mod, g, w_router, seq_len):
    n, d = x2.shape
    n_exp = w_router.shape[1]
    tm = min(ROW_TILE, seq_len)
    wr = jnp.zeros((d, LANES), F32).at[:, :n_exp].set(w_router)
    row = lambda w: pl.BlockSpec((tm, w), lambda i: (i, 0))
    return pl.pallas_call(
        functools.partial(_router_kernel, n_exp=n_exp),
        grid=(n // tm,),
        in_specs=[
            row(d),
            pl.BlockSpec((1, N_MOD, d), lambda i: (i * tm // seq_len, 0, 0)),
            _const_spec((1, d)), _const_spec((d, LANES)),
        ],
        out_specs=[row(d), row(LANES)],
        out_shape=[jax.ShapeDtypeStruct((n, d), BF16), jax.ShapeDtypeStruct((n, LANES), F32)],
        compiler_params=_params(("parallel",)),
        name="router",
    )(x2, mod, g.reshape(1, d), wr)


def _count_le(sorted_vals, v):
    return jnp.sum((sorted_vals[None, :] <= v[:, None]).astype(I32), axis=1)


def _moe_plan(route, n_exp):
    n = route.shape[0]
    te, tg, tc = MOE_EXPERT_TILE, MOE_SLOT_TILE, MOE_TOKEN_CHUNK
    assert n % MOE_GATHER_CHUNK == 0 and MOE_GATHER_CHUNK % tc == 0 and te % tg == 0
    idx = route[:, :TOP_K].astype(I32)
    wts = route[:, TOP_K:2 * TOP_K]
    ex = jnp.arange(n_exp, dtype=I32)
    hit = idx[:, :, None] == ex[None, None, :]
    sel = hit.any(axis=1)
    gate = jnp.sum(jnp.where(hit, wts[:, :, None], 0.0), axis=1)
    seli = sel.astype(I32)
    cnt = seli.sum(axis=0)
    padded = (cnt + te - 1) // te * te
    ends = jnp.cumsum(padded)
    off = ends - padded
    total = ends[-1]
    rank = jnp.cumsum(seli, axis=0) - seli
    pos = jnp.where(sel, off[None, :] + rank, -1)
    n_slots = TOP_K * n + n_exp * te
    n_tg, n_te, n_ch = n_slots // tg, n_slots // te, n // tc
    g_max = n_exp * n_ch + n_tg

    posc = pos.reshape(n_ch, tc, n_exp)
    hasc = (posc >= 0).any(axis=1)
    pmin = jnp.where(posc >= 0, posc, n_slots).min(axis=1)
    pmax = posc.max(axis=1)

    te_base = jnp.arange(n_te, dtype=I32) * te
    te_valid = (te_base < total).astype(I32)
    last_valid = jnp.maximum(total // te - 1, 0)
    te_exp = jnp.minimum(_count_le(ends, jnp.minimum(te_base, last_valid * te)), n_exp - 1).astype(I32)

    tg_base = jnp.arange(n_tg, dtype=I32) * tg
    tg_exp = jnp.minimum(_count_le(ends, tg_base), n_exp - 1).astype(I32)
    used_end = off[tg_exp] + cnt[tg_exp]
    has = tg_base < used_end
    ratio = MOE_GATHER_CHUNK // tc
    n_gc = n_ch // ratio
    has_g = hasc.reshape(n_gc, ratio, n_exp).any(axis=1)
    pmin_g = pmin.reshape(n_gc, ratio, n_exp).min(axis=1)
    pmax_g = pmax.reshape(n_gc, ratio, n_exp).max(axis=1)
    ch = jnp.arange(n_gc, dtype=I32)[None, :]
    has_e = jnp.take(has_g.T, tg_exp, axis=0)
    first_c = jnp.where(has_e & (jnp.take(pmax_g.T, tg_exp, axis=0) >= tg_base[:, None]), ch, n_gc).min(axis=1)
    last_c = jnp.where(has_e & (jnp.take(pmin_g.T, tg_exp, axis=0) < tg_base[:, None] + tg), ch, -1).max(axis=1)
    c_lo = jnp.where(has, first_c, 0)
    c_hi = jnp.where(has, last_c, 0)
    n_it = c_hi - c_lo + 1
    start = jnp.cumsum(n_it) - n_it
    tot_g = start[-1] + n_it[-1]
    gg = jnp.arange(n_exp * n_gc + n_tg, dtype=I32)
    gc = jnp.minimum(gg, tot_g - 1)
    s_of = (_count_le(start, gc) - 1).astype(I32)
    ga_valid = (gg < tot_g).astype(I32)
    ga_tile = s_of
    ga_chunk = (c_lo[s_of] + gc - start[s_of]).astype(I32)
    ga_first = ((gc == start[s_of]) & (gg < tot_g)).astype(I32)
    g = jnp.arange(g_max, dtype=I32)

    s_lo = jnp.where(hasc, pmin // tg, 0).reshape(-1)
    s_hi = jnp.where(hasc, pmax // tg, 0).reshape(-1)
    ccnt = jnp.where(hasc.reshape(-1), s_hi - s_lo + 1, 0)
    cend = jnp.cumsum(ccnt)
    cstart = cend - ccnt
    tot_c = cend[-1]
    hc = jnp.minimum(g, tot_c - 1)
    pair = (_count_le(cstart, hc) - 1).astype(I32)
    co_valid = (g < tot_c).astype(I32)
    co_chunk = (pair // n_exp).astype(I32)
    co_exp = (pair % n_exp).astype(I32)
    co_tile = (s_lo[pair] + hc - cstart[pair]).astype(I32)
    chunk_start = cstart.reshape(n_ch, n_exp)[:, 0]
    chunk_end = cend.reshape(n_ch, n_exp)[:, -1]
    co_first = ((hc == chunk_start[co_chunk]) & (g < tot_c)).astype(I32)
    co_last = ((hc == chunk_end[co_chunk] - 1) & (g < tot_c)).astype(I32)

    pos_t = pos.T.astype(I32)
    return dict(n_slots=n_slots, g_max=g_max, te_exp=te_exp, te_valid=te_valid, tg_exp=tg_exp,
                ga=(ga_tile, ga_chunk, ga_first, ga_valid), co=(co_tile, co_chunk, co_exp, co_first, co_last, co_valid),
                pos_t=pos_t, pos_f=pos.astype(F32), gate=gate)


def _gather_kernel(it_tile, it_chunk, it_first, it_valid, tg_exp, h_ref, pos_ref, o_ref):
    g = pl.program_id(0)
    tg, tc = MOE_SLOT_TILE, MOE_GATHER_CHUNK

    @pl.when(it_valid[g] == 1)
    def _():
        s = it_tile[g]
        e = tg_exp[s]
        pos_row = pos_ref[pl.ds(e, 1), :]
        slot = s * tg + lax.broadcasted_iota(I32, (tg, tc), 0)
        onehot = jnp.where(pos_row == slot, 1.0, 0.0).astype(BF16)
        r = _dot(onehot, h_ref[...]).astype(BF16)

        @pl.when(it_first[g] == 1)
        def _():
            o_ref[...] = r

        @pl.when(it_first[g] == 0)
        def _():
            o_ref[...] += r


def _moe_gather(h_bf, plan):
    n, d = h_bf.shape
    n_exp = plan["pos_t"].shape[0]
    tg, tc = MOE_SLOT_TILE, MOE_GATHER_CHUNK
    grid_spec = pltpu.PrefetchScalarGridSpec(
        num_scalar_prefetch=5,
        grid=(plan["ga"][0].shape[0],),
        in_specs=[
            pl.BlockSpec((tc, d), lambda g, t, c, f, v, te: (c[g], 0)),
            pl.BlockSpec((n_exp, tc), lambda g, t, c, f, v, te: (0, c[g])),
        ],
        out_specs=pl.BlockSpec((tg, d), lambda g, t, c, f, v, te: (t[g], 0)),
    )
    return pl.pallas_call(
        _gather_kernel,
        grid_spec=grid_spec,
        out_shape=jax.ShapeDtypeStruct((plan["n_slots"], d), BF16),
        compiler_params=_params(("arbitrary",)),
        name="moe_gather",
    )(*plan["ga"], plan["tg_exp"], h_bf, plan["pos_t"])


def _experts_kernel(te_exp, te_valid, h_ref, wg_ref, wu_ref, wd_ref, y_ref, acc_ref, *, sub):
    i = pl.program_id(0)
    f = pl.program_id(1)

    @pl.when(f == 0)
    def _():
        acc_ref[...] = jnp.zeros_like(acc_ref)

    @pl.when(te_valid[i] == 1)
    def _():
        h = h_ref[...]
        for c in range(0, wg_ref.shape[2], sub):
            g = _dot(h, wg_ref[0, :, c:c + sub])
            u = _dot(h, wu_ref[0, :, c:c + sub])
            a = (g * jax.nn.sigmoid(g) * u).astype(BF16)
            acc_ref[...] += _dot(a, wd_ref[0, c:c + sub, :])

    @pl.when(f == pl.num_programs(1) - 1)
    def _():
        y_ref[...] = acc_ref[...].astype(BF16)


def _moe_experts(hs, wg, wu, wd, plan):
    n_slots, d = hs.shape
    ff = wg.shape[2]
    te = MOE_EXPERT_TILE
    sub = MOE_FF_CHUNK if ff % (2 * MOE_FF_CHUNK) == 0 else LANES
    tf = ff // 2 if (ff // 2) % sub == 0 else ff
    grid_spec = pltpu.PrefetchScalarGridSpec(
        num_scalar_prefetch=2,
        grid=(n_slots // te, ff // tf),
        in_specs=[
            pl.BlockSpec((te, d), lambda i, f, e, v: (i, 0)),
            pl.BlockSpec((1, d, tf), lambda i, f, e, v: (e[i], 0, f * v[i])),
            pl.BlockSpec((1, d, tf), lambda i, f, e, v: (e[i], 0, f * v[i])),
            pl.BlockSpec((1, tf, d), lambda i, f, e, v: (e[i], f * v[i], 0)),
        ],
        out_specs=pl.BlockSpec((te, d), lambda i, f, e, v: (i, 0)),
        scratch_shapes=[pltpu.VMEM((te, d), F32)],
    )
    return pl.pallas_call(
        functools.partial(_experts_kernel, sub=sub),
        grid_spec=grid_spec,
        out_shape=jax.ShapeDtypeStruct((n_slots, d), BF16),
        compiler_params=_params(("arbitrary", "arbitrary")),
        name="moe_experts",
    )(plan["te_exp"], plan["te_valid"], hs, wg, wu, wd)


def _combine_kernel(it_tile, it_chunk, it_exp, it_first, it_last, it_valid,
                    y_ref, pos_ref, gate_ref, x_ref, mod_ref, fg_ref, o_ref, acc_ref, *, final):
    g = pl.program_id(0)
    tg, tc = MOE_SLOT_TILE, MOE_TOKEN_CHUNK

    @pl.when(it_first[g] == 1)
    def _():
        acc_ref[...] = jnp.zeros_like(acc_ref)

    @pl.when(it_valid[g] == 1)
    def _():
        e = it_exp[g]
        lane = lax.broadcasted_iota(I32, pos_ref.shape, 1)
        pos_col = jnp.sum(jnp.where(lane == e, pos_ref[...], 0.0), axis=-1, keepdims=True)
        w_col = jnp.sum(jnp.where(lane == e, gate_ref[...], 0.0), axis=-1, keepdims=True)
        slot = (it_tile[g] * tg + lax.broadcasted_iota(I32, (tc, tg), 1)).astype(F32)
        onehot = jnp.where(pos_col == slot, 1.0, 0.0).astype(BF16)
        acc_ref[...] += w_col * _dot(onehot, y_ref[...])

    @pl.when(it_last[g] == 1)
    def _():
        out = x_ref[...] + mod_ref[0, 5:6, :] * acc_ref[...]
        o_ref[...] = _rmsnorm(out, fg_ref[...]) if final else out


def _moe_combine(y, x2, mod, final_g, plan, seq_len, final):
    n, d = x2.shape
    n_exp = plan["pos_t"].shape[0]
    tg, tc = MOE_SLOT_TILE, MOE_TOKEN_CHUNK
    assert seq_len % tc == 0
    grid_spec = pltpu.PrefetchScalarGridSpec(
        num_scalar_prefetch=6,
        grid=(plan["g_max"],),
        in_specs=[
            pl.BlockSpec((tg, d), lambda g, t, c, *_: (t[g], 0)),
            pl.BlockSpec((tc, n_exp), lambda g, t, c, *_: (c[g], 0)),
            pl.BlockSpec((tc, n_exp), lambda g, t, c, *_: (c[g], 0)),
            pl.BlockSpec((tc, d), lambda g, t, c, *_: (c[g], 0)),
            pl.BlockSpec((1, N_MOD, d), lambda g, t, c, *_: (c[g] * tc // seq_len, 0, 0)),
            pl.BlockSpec((1, d), lambda g, *_: (0, 0)),
        ],
        out_specs=pl.BlockSpec((tc, d), lambda g, t, c, *_: (c[g], 0)),
        scratch_shapes=[pltpu.VMEM((tc, d), F32)],
    )
    return pl.pallas_call(
        functools.partial(_combine_kernel, final=final),
        grid_spec=grid_spec,
        out_shape=jax.ShapeDtypeStruct((n, d), F32),
        compiler_params=_params(("arbitrary",)),
        name="moe_combine",
    )(*plan["co"], y, plan["pos_f"], plan["gate"], x2, mod, final_g.reshape(1, d))


def _token_mix(x2, mod, l, seq_len, bt, p, kv_ctx, tabs, bias):
    a0, b0, q, k, v, zu, zb, zc, gates = _inproj(x2, mod, p["norm1_g"][l], p["w_in"][l], tabs[3], seq_len)
    fm = _fourier(a0, b0, bt, seq_len, *tabs[:3])
    if kv_ctx is None:
        attn = _ctxattn(q, k, v, bt, seq_len)
    else:
        attn = _natten(q, k, v, kv_ctx[0], kv_ctx[1], bias, bt, seq_len, kv_ctx[0].shape[0] // bt)
    out = _merge(fm, attn, zu, zb, zc, gates, x2, mod, p["conv_w"][l], p["w_fourier"][l], p["w_na"][l],
                 p["w_conv_out"][l], p["w_out"][l], seq_len)
    return out, (k, v)


def kernel(x, c, ctx, c_ctx, norm1_g, norm2_g, w_ada, b_ada, w_in, conv_w, na_rpb, w_fourier, w_na, w_conv_out, w_out, ffn_w_gate, ffn_w_up, ffn_w_down, moe_router, moe_w_gate, moe_w_up, moe_w_down, final_g):
    bt, seq, d = x.shape
    lc = ctx.shape[1]
    depth = w_in.shape[0]
    n_exp = moe_router.shape[-1]
    bf = lambda w: w.astype(BF16)
    p = dict(norm1_g=norm1_g, w_in=bf(w_in), conv_w=conv_w, w_fourier=bf(w_fourier), w_na=bf(w_na),
             w_conv_out=bf(w_conv_out), w_out=bf(w_out))
    ffn_g, ffn_u, ffn_d = bf(ffn_w_gate), bf(ffn_w_up), bf(ffn_w_down)
    moe_g, moe_u, moe_d = bf(moe_w_gate), bf(moe_w_up), bf(moe_w_down)

    rows8 = -(-(bt + 1) // 8) * 8
    cvec = jnp.zeros((rows8, d), F32).at[:bt].set(c).at[bt].set(c_ctx)
    mods = _adaln(cvec, w_ada, b_ada)

    tabs_x = _fft_tables(seq)
    tabs_c = _fft_tables(lc)
    x2 = x.reshape(bt * seq, d)
    c2 = ctx.reshape(bt * lc, d)
    for l in range(depth):
        last = l == depth - 1
        mod_x = mods[l, :bt].reshape(bt, N_MOD, d)
        mod_c = jnp.broadcast_to(mods[l, bt].reshape(1, N_MOD, d), (bt, N_MOD, d))
        bias = _bias_tables(na_rpb[l], seq // GRID_W)
        dense = l % 2 == 0
        j = l // 2
        if last:
            outs = _inproj(c2, mod_c, norm1_g[l], p["w_in"][l], tabs_c[3], lc)
            kv_ctx = (outs[3], outs[4])
        else:
            c_mid, kv_ctx = _token_mix(c2, mod_c, l, lc, bt, p, None, tabs_c, None)
            if not dense:
                raise NotImplementedError("MoE channel mixer on context tokens (only reached when depth > 2)")
            c2 = _ffn(c_mid, mod_c, norm2_g[l], ffn_g[j], ffn_u[j], ffn_d[j], final_g, lc, False)
        x_mid, _ = _token_mix(x2, mod_x, l, seq, bt, p, kv_ctx, tabs_x, bias)
        if dense:
            x2 = _ffn(x_mid, mod_x, norm2_g[l], ffn_g[j], ffn_u[j], ffn_d[j], final_g, seq, last)
        else:
            h_bf, route = _router(x_mid, mod_x, norm2_g[l], moe_router[j], seq)
            plan = _moe_plan(route, n_exp)
            hs = _moe_gather(h_bf, plan)
            y = _moe_experts(hs, moe_g[j], moe_u[j], moe_d[j], plan)
            x2 = _moe_combine(y, x_mid, mod_x, final_g, plan, seq, last)
    return x2.reshape(bt, seq, d)
```

```python
import functools

import numpy as np
import jax
import jax.numpy as jnp
from jax import lax
from jax.experimental import pallas as pl
from jax.experimental.pallas import tpu as pltpu

F32 = jnp.float32
BF16 = jnp.bfloat16
I32 = jnp.int32

EPS = 1e-6
N_MOD = 6
GRID_W = 64
WIN_H = 8
WIN_W = 16
F_GROUPS = 4
F_GROUP_DIM = 64
F_WIDTH = F_GROUPS * F_GROUP_DIM
NA_HEADS = 8
NA_HEAD_DIM = 64
NA_WIDTH = NA_HEADS * NA_HEAD_DIM
CONV_WIDTH = 256
CONV_K = 3
N_BRANCH = 3
TOP_K = 2
Q_OFF = F_WIDTH
K_OFF = Q_OFF + NA_WIDTH
V_OFF = K_OFF + NA_WIDTH
U_OFF = V_OFF + NA_WIDTH
B_OFF = U_OFF + CONV_WIDTH
C_OFF = B_OFF + CONV_WIDTH
G_OFF = C_OFF + CONV_WIDTH

LANES = 128
BF16_SUBLANES = 16
VMEM_LIMIT_BYTES = 56 * 1024 * 1024

ROW_TILE = 1024
Q_ROWS = 8
KEY_ROWS = 16
MASK_VALUE = -1e30
MOE_EXPERT_TILE = 512
MOE_SLOT_TILE = 256
MOE_TOKEN_CHUNK = 512
MOE_GATHER_CHUNK = 1024
MOE_FF_CHUNK = 256


def _params(sem):
    return pltpu.CompilerParams(dimension_semantics=sem, vmem_limit_bytes=VMEM_LIMIT_BYTES)


def _const_spec(shape):
    nd = len(shape)
    return pl.BlockSpec(shape, lambda *_: (0,) * nd, pipeline_mode=pl.Buffered(1))


def _dot(a, b):
    return jnp.dot(a, b, preferred_element_type=F32)


def _split_bf16(x):
    hi = x.astype(BF16)
    lo = (x - hi.astype(F32)).astype(BF16)
    return hi, lo


def _dot_split(a, w):
    ah, al = _split_bf16(a)
    wh, wl = _split_bf16(w)
    return _dot(ah, wh) + (_dot(ah, wl) + _dot(al, wh))


def _norm_mod(x, g, shift, scale):
    ms = jnp.mean(x * x, axis=-1, keepdims=True)
    y = x * lax.rsqrt(ms + EPS) * g
    return y * (1.0 + scale) + shift


def _rmsnorm(x, g):
    ms = jnp.mean(x * x, axis=-1, keepdims=True)
    return x * lax.rsqrt(ms + EPS) * g


def _adaln_kernel(c_ref, w_ref, b_ref, o_ref):
    a = c_ref[...]
    a = a * jax.nn.sigmoid(a)
    o_ref[0] = _dot_split(a, w_ref[0]) + b_ref[0]


def _adaln(cvec, w_ada, b_ada):
    depth, d, width = w_ada.shape
    rows = cvec.shape[0]
    tn = width // 4
    return pl.pallas_call(
        _adaln_kernel,
        grid=(depth, width // tn),
        in_specs=[
            pl.BlockSpec((rows, d), lambda l, j: (0, 0)),
            pl.BlockSpec((1, d, tn), lambda l, j: (l, 0, j)),
            pl.BlockSpec((1, 1, tn), lambda l, j: (l, 0, j)),
        ],
        out_specs=pl.BlockSpec((1, rows, tn), lambda l, j: (l, 0, j)),
        out_shape=jax.ShapeDtypeStruct((depth, rows, width), F32),
        compiler_params=_params(("arbitrary", "arbitrary")),
        name="adaln",
    )(cvec, w_ada, b_ada.reshape(depth, 1, width))


def _inproj_kernel(x_ref, mod_ref, g_ref, w_ref, bd_ref,
                   a0_ref, b0_ref, q_ref, k_ref, v_ref, u_ref, b_ref, c_ref, gate_ref, *, d):
    h = _norm_mod(x_ref[...], g_ref[...], mod_ref[0, 0:1, :], mod_ref[0, 1:2, :]).astype(BF16)

    def proj(c0, c1):
        return _dot(h, w_ref[:, c0:c1])

    zf = proj(0, Q_OFF).astype(BF16)
    ab = _dot(zf, bd_ref[...])
    a0_ref[...] = ab[:, :F_WIDTH].astype(BF16)
    b0_ref[...] = ab[:, F_WIDTH:].astype(BF16)
    q_ref[...] = (proj(Q_OFF, K_OFF) * (NA_HEAD_DIM ** -0.5)).astype(BF16)
    k_ref[...] = proj(K_OFF, V_OFF).astype(BF16)
    v_ref[...] = proj(V_OFF, U_OFF).astype(BF16)
    u_ref[...] = proj(U_OFF, B_OFF).astype(BF16)
    b_ref[...] = proj(B_OFF, C_OFF).astype(BF16)
    c_ref[...] = proj(C_OFF, G_OFF).astype(BF16)
    step = 512 if d % 512 == 0 else d
    for j in range(0, N_BRANCH * d, step):
        gate_ref[:, j:j + step] = jax.nn.sigmoid(proj(G_OFF + j, G_OFF + j + step)).astype(BF16)


def _inproj(x2, mod, g, w_bf, bd, seq_len):
    n, d = x2.shape
    tm = min(ROW_TILE, seq_len)
    pw = w_bf.shape[1]
    widths = [F_WIDTH, F_WIDTH, NA_WIDTH, NA_WIDTH, NA_WIDTH, CONV_WIDTH, CONV_WIDTH, CONV_WIDTH, N_BRANCH * d]
    row = lambda w: pl.BlockSpec((tm, w), lambda i: (i, 0))
    return pl.pallas_call(
        functools.partial(_inproj_kernel, d=d),
        grid=(n // tm,),
        in_specs=[
            row(d),
            pl.BlockSpec((1, N_MOD, d), lambda i: (i * tm // seq_len, 0, 0)),
            _const_spec((1, d)),
            _const_spec((d, pw)),
            _const_spec((F_WIDTH, 2 * F_WIDTH)),
        ],
        out_specs=[row(w) for w in widths],
        out_shape=[jax.ShapeDtypeStruct((n, w), BF16) for w in widths],
        compiler_params=_params(("parallel",)),
        name="inproj",
    )(x2, mod, g.reshape(1, d), w_bf, bd)


def _fft_split(seq_len):
    b = 64 if seq_len >= 1024 else 16
    return seq_len // b, b


def _fft_tables(seq_len):
    a_pts, b_pts = _fft_split(seq_len)
    ia = np.arange(a_pts)
    ang = 2.0 * np.pi * ((ia[:, None] * ia[None, :]) % a_pts) / a_pts
    c, s = np.cos(ang) / np.sqrt(a_pts), np.sin(ang) / np.sqrt(a_pts)
    w1 = np.block([[c, -s], [-s, -c]])
    ib = np.arange(b_pts)
    num = (ib[None, :, None] * ib[None, None, :] * a_pts + ib[None, None, :] * ia[:, None, None]) % seq_len
    th = 2.0 * np.pi * num / seq_len
    mc, ms = np.cos(th) / np.sqrt(b_pts), np.sin(th) / np.sqrt(b_pts)
    ic = np.arange(F_GROUP_DIM)
    angc = 2.0 * np.pi * ((ic[:, None] * ic[None, :]) % F_GROUP_DIM) / F_GROUP_DIM
    eye = np.eye(F_GROUPS)
    bd = np.concatenate([np.kron(eye, np.cos(angc)), np.kron(eye, np.sin(angc))], axis=1) / np.sqrt(F_GROUP_DIM)
    return tuple(jnp.asarray(t, F32).astype(BF16) for t in (w1, mc, ms, bd))


def _fft1_kernel(a_ref, b_ref, w_ref, o_ref, *, a_pts):
    x = jnp.concatenate([a_ref[0], b_ref[0]], axis=0)
    t = _dot(w_ref[...], x)
    o_ref[0, 0] = t[:a_pts].astype(BF16)
    o_ref[0, 1] = t[a_pts:].astype(BF16)


def _fft2_kernel(tr_ref, ti_ref, mc_ref, ms_ref, o_ref, *, ac):
    for j in range(ac):
        y = _dot(mc_ref[j], tr_ref[0, 0, j]) + _dot(ms_ref[j], ti_ref[0, 0, j])
        o_ref[0, j] = y.astype(BF16)


def _fourier(a0, b0, bt, seq_len, w1, mc, ms):
    a_pts, b_pts = _fft_split(seq_len)
    c = F_WIDTH
    bc = b_pts * c
    cw = min(2048, bc)
    view = lambda z: z.reshape(bt, a_pts, bc)
    t = pl.pallas_call(
        functools.partial(_fft1_kernel, a_pts=a_pts),
        grid=(bt, bc // cw),
        in_specs=[
            pl.BlockSpec((1, a_pts, cw), lambda b, j: (b, 0, j)),
            pl.BlockSpec((1, a_pts, cw), lambda b, j: (b, 0, j)),
            _const_spec((2 * a_pts, 2 * a_pts)),
        ],
        out_specs=pl.BlockSpec((1, 2, a_pts, cw), lambda b, j: (b, 0, 0, j)),
        out_shape=jax.ShapeDtypeStruct((bt, 2, a_pts, bc), BF16),
        compiler_params=_params(("parallel", "parallel")),
        name="fft1",
    )(view(a0), view(b0), w1)
    t5 = t.reshape(bt, 2, a_pts, b_pts, c)
    ac = min(a_pts, 16)
    out = pl.pallas_call(
        functools.partial(_fft2_kernel, ac=ac),
        grid=(a_pts // ac, bt),
        in_specs=[
            pl.BlockSpec((1, 1, ac, b_pts, c), lambda i, b: (b, 0, i, 0, 0)),
            pl.BlockSpec((1, 1, ac, b_pts, c), lambda i, b: (b, 1, i, 0, 0)),
            pl.BlockSpec((ac, b_pts, b_pts), lambda i, b: (i, 0, 0)),
            pl.BlockSpec((ac, b_pts, b_pts), lambda i, b: (i, 0, 0)),
        ],
        out_specs=pl.BlockSpec((1, ac, b_pts, c), lambda i, b: (b, i, 0, 0)),
        out_shape=jax.ShapeDtypeStruct((bt, a_pts, b_pts, c), BF16),
        compiler_params=_params(("parallel", "parallel")),
        name="fft2",
    )(t5, t5, mc, ms)
    return out.reshape(bt * a_pts, bc)


def _window_rows(rows):
    r0s = np.array([0, Q_ROWS, rows - Q_ROWS])
    i = np.arange(Q_ROWS)
    j = np.arange(KEY_ROWS)
    r = r0s[:, None, None] + i[None, :, None]
    kr = r0s[:, None, None] - WIN_H // 2 + j[None, None, :]
    base = np.clip(r - WIN_H // 2, 0, rows - WIN_H)
    ok = (kr >= 0) & (kr < rows) & (kr >= base) & (kr < base + WIN_H)
    return tuple(tuple(tuple(bool(b) for b in row) for row in blk) for blk in ok)


def _biastab_kernel(rpb_ref, o_ref, *, ok_r):
    w = GRID_W
    cq = lax.broadcasted_iota(I32, (w, LANES), 0)
    kc = lax.broadcasted_iota(I32, (w, LANES), 1)
    cs = jnp.clip(cq - WIN_W // 2, 0, w - WIN_W)
    ok_c = (kc >= cs) & (kc < cs + WIN_W)
    low = kc < w
    masked = jnp.full((w, LANES), MASK_VALUE, F32)
    lo, hi = [], []
    for dr in range(2 * WIN_H - 1):
        xb = jnp.broadcast_to(rpb_ref[0, dr:dr + 1, :], (w, LANES))
        t = pltpu.roll(xb, LANES - (WIN_W - 1), 1, stride=1, stride_axis=0)
        t = jnp.where(ok_c, t, MASK_VALUE)
        lo.append(t)
        hi.append(pltpu.roll(t, w, 1))
    for v in range(3):
        for i in range(Q_ROWS):
            for jj in range(KEY_ROWS // 2):
                j0 = 2 * jj
                d0 = j0 - i + WIN_H // 2 - 1
                a = lo[d0] if ok_r[v][i][j0] else masked
                b = hi[d0 + 1] if ok_r[v][i][j0 + 1] else masked
                blk = masked if (a is masked and b is masked) else jnp.where(low, a, b)
                o_ref[v, 0, i * w:(i + 1) * w, jj * LANES:(jj + 1) * LANES] = blk.astype(BF16)


def _bias_tables(rpb, rows):
    n_dr, n_dc = rpb.shape[1], rpb.shape[2]
    rp = jnp.zeros((NA_HEADS, 2 * WIN_H, LANES), F32).at[:, :n_dr, :n_dc].set(rpb.astype(F32))
    qb, wk = Q_ROWS * GRID_W, KEY_ROWS * GRID_W
    return pl.pallas_call(
        functools.partial(_biastab_kernel, ok_r=_window_rows(rows)),
        grid=(NA_HEADS,),
        in_specs=[pl.BlockSpec((1, 2 * WIN_H, LANES), lambda h: (h, 0, 0))],
        out_specs=pl.BlockSpec((3, 1, qb, wk), lambda h: (0, h, 0, 0)),
        out_shape=jax.ShapeDtypeStruct((3, NA_HEADS, qb, wk), BF16),
        compiler_params=_params(("parallel",)),
        name="biastab",
    )(rp)


def _softmax_pv(s_parts, v_parts):
    m = s_parts[0].max(axis=-1, keepdims=True)
    for s in s_parts[1:]:
        m = jnp.maximum(m, s.max(axis=-1, keepdims=True))
    l = None
    pv = None
    for s, v in zip(s_parts, v_parts):
        p = jnp.exp(s - m)
        ls = p.sum(axis=-1, keepdims=True)
        o = _dot(p.astype(BF16), v)
        l = ls if l is None else l + ls
        pv = o if pv is None else pv + o
    return pv * (1.0 / l)


def _natten_kernel(q_ref, kp_ref, kc_ref, kn_ref, vp_ref, vc_ref, vn_ref, kx_ref, vx_ref, bias_ref,
                   o_ref, kw_ref, vw_ref, *, lc):
    qb = Q_ROWS * GRID_W
    halo = (KEY_ROWS - Q_ROWS) // 2 * GRID_W
    wk = KEY_ROWS * GRID_W
    for dst, (p, c, n, x) in ((kw_ref, (kp_ref, kc_ref, kn_ref, kx_ref)), (vw_ref, (vp_ref, vc_ref, vn_ref, vx_ref))):
        dst[0:halo] = p[qb - halo:qb]
        dst[halo:halo + qb] = c[...]
        dst[halo + qb:wk] = n[0:halo]
        dst[wk:wk + lc] = x[...]
    lane = lax.broadcasted_iota(I32, (qb, LANES), 1)
    low = lane < NA_HEAD_DIM
    for hp in range(NA_HEADS // 2):
        sl = slice(LANES * hp, LANES * (hp + 1))
        qp = q_ref[:, sl]
        kw = kw_ref[:, sl]
        vw = vw_ref[:, sl]
        outs = []
        for sub in range(2):
            qm = jnp.where(low if sub == 0 else jnp.logical_not(low), qp, jnp.zeros_like(qp))
            s = lax.dot_general(qm, kw, (((1,), (1,)), ((), ())), preferred_element_type=F32)
            s_win = s[:, :wk] + bias_ref[0, 2 * hp + sub].astype(F32)
            outs.append(_softmax_pv([s_win, s[:, wk:]], [vw[:wk], vw[wk:]]))
        o_ref[:, sl] = jnp.where(low, outs[0], outs[1]).astype(BF16)


def _natten(q, k, v, kx, vx, bias, bt, seq_len, lc):
    n = q.shape[0]
    qb = Q_ROWS * GRID_W
    nb = seq_len // qb
    assert seq_len % qb == 0 and nb >= 2, "neighbourhood attention kernel needs >= 16 grid rows, a multiple of 8"
    wk = KEY_ROWS * GRID_W
    cur = lambda b, r: (b * nb + r, 0)
    prev = lambda b, r: (b * nb + jnp.maximum(r - 1, 0), 0)
    nxt = lambda b, r: (b * nb + jnp.minimum(r + 1, nb - 1), 0)
    blk = lambda im: pl.BlockSpec((qb, NA_WIDTH), im)
    variant = lambda b, r: (jnp.where(r == 0, 0, jnp.where(r == nb - 1, 2, 1)), 0, 0, 0)
    return pl.pallas_call(
        functools.partial(_natten_kernel, lc=lc),
        grid=(bt, nb),
        in_specs=[
            blk(cur), blk(prev), blk(cur), blk(nxt), blk(prev), blk(cur), blk(nxt),
            pl.BlockSpec((lc, NA_WIDTH), lambda b, r: (b, 0)),
            pl.BlockSpec((lc, NA_WIDTH), lambda b, r: (b, 0)),
            pl.BlockSpec((1, NA_HEADS, qb, wk), variant),
        ],
        out_specs=blk(cur),
        out_shape=jax.ShapeDtypeStruct((n, NA_WIDTH), BF16),
        scratch_shapes=[pltpu.VMEM((wk + lc, NA_WIDTH), BF16), pltpu.VMEM((wk + lc, NA_WIDTH), BF16)],
        compiler_params=_params(("parallel", "arbitrary")),
        name="natten",
    )(q, k, k, k, v, v, v, kx, vx, bias)


def _ctxattn_kernel(q_ref, k_ref, v_ref, o_ref):
    lc = q_ref.shape[0]
    lane = lax.broadcasted_iota(I32, (lc, LANES), 1)
    low = lane < NA_HEAD_DIM
    for hp in range(NA_HEADS // 2):
        sl = slice(LANES * hp, LANES * (hp + 1))
        qp = q_ref[:, sl]
        kw = k_ref[:, sl]
        vw = v_ref[:, sl]
        outs = []
        for sub in range(2):
            qm = jnp.where(low if sub == 0 else jnp.logical_not(low), qp, jnp.zeros_like(qp))
            s = lax.dot_general(qm, kw, (((1,), (1,)), ((), ())), preferred_element_type=F32)
            outs.append(_softmax_pv([s], [vw]))
        o_ref[:, sl] = jnp.where(low, outs[0], outs[1]).astype(BF16)


def _ctxattn(q, k, v, bt, lc):
    blk = pl.BlockSpec((lc, NA_WIDTH), lambda b: (b, 0))
    return pl.pallas_call(
        _ctxattn_kernel,
        grid=(bt,),
        in_specs=[blk, blk, blk],
        out_specs=blk,
        out_shape=jax.ShapeDtypeStruct(q.shape, BF16),
        compiler_params=_params(("parallel",)),
        name="ctxattn",
    )(q, k, v)


def _merge_kernel(fm_ref, at_ref, u_ref, b_ref, c_ref, up_ref, cp_ref, un_ref, cn_ref, gate_ref, x_ref, mod_ref,
                  cw_ref, wf_ref, wa_ref, wc_ref, wo_ref, o_ref, pre_ref, *, tm, d, a_pts, seq_len):
    i = pl.program_id(0)
    t0 = i * tm
    first = (t0 % seq_len) == 0
    last = ((t0 + tm) % seq_len) == 0
    p = c_ref[...].astype(F32) * u_ref[...].astype(F32)
    hs = BF16_SUBLANES
    p_above = cp_ref[hs - 1:hs, :].astype(F32) * up_ref[hs - 1:hs, :].astype(F32)
    p_below = cn_ref[0:1, :].astype(F32) * un_ref[0:1, :].astype(F32)
    p_above = jnp.where(first, 0.0, p_above)
    p_below = jnp.where(last, 0.0, p_below)
    row = lax.broadcasted_iota(I32, (tm, CONV_WIDTH), 0)
    p_dn = jnp.where(row == 0, p_above, pltpu.roll(p, 1, 0))
    p_up = jnp.where(row == tm - 1, p_below, pltpu.roll(p, tm - 1, 0))
    cv = cw_ref[0:1, :] * p_dn + cw_ref[1:2, :] * p + cw_ref[2:3, :] * p_up
    y_c = _dot((b_ref[...].astype(F32) * cv).astype(BF16), wc_ref[...])
    y_a = _dot(at_ref[...], wa_ref[...])
    pre_ref[...] = (gate_ref[:, d:2 * d].astype(F32) * y_a + gate_ref[:, 2 * d:3 * d].astype(F32) * y_c)
    for j in range(tm // a_pts):
        rs = slice(j * a_pts, (j + 1) * a_pts)
        y_f = _dot(fm_ref[:, j * F_WIDTH:(j + 1) * F_WIDTH], wf_ref[...])
        pre_ref[rs, :] += gate_ref[rs, 0:d].astype(F32) * y_f
    mix = _dot(pre_ref[...].astype(BF16), wo_ref[...])
    o_ref[...] = x_ref[...] + mod_ref[0, 2:3, :] * mix


def _merge(fm, attn, zu, zb, zc, gates, x2, mod, conv_w, wf, wa, wc, wo, seq_len):
    n, d = x2.shape
    tm = min(ROW_TILE, seq_len)
    a_pts, b_pts = _fft_split(seq_len)
    assert tm % a_pts == 0
    hs = BF16_SUBLANES
    nh = n // hs
    per = tm // hs
    row = lambda w: pl.BlockSpec((tm, w), lambda i: (i, 0))
    above = pl.BlockSpec((hs, CONV_WIDTH), lambda i: (jnp.maximum(i * per - 1, 0), 0))
    below = pl.BlockSpec((hs, CONV_WIDTH), lambda i: (jnp.minimum((i + 1) * per, nh - 1), 0))
    tiles_per_seq = seq_len // tm
    sub = tm // a_pts
    fm_spec = pl.BlockSpec((a_pts, sub * F_WIDTH), lambda i: (i // tiles_per_seq, i % tiles_per_seq))
    return pl.pallas_call(
        functools.partial(_merge_kernel, tm=tm, d=d, a_pts=a_pts, seq_len=seq_len),
        grid=(n // tm,),
        in_specs=[
            fm_spec, row(NA_WIDTH), row(CONV_WIDTH), row(CONV_WIDTH), row(CONV_WIDTH),
            above, above, below, below,
            row(N_BRANCH * d), row(d),
            pl.BlockSpec((1, N_MOD, d), lambda i: (i * tm // seq_len, 0, 0)),
            _const_spec((CONV_K, CONV_WIDTH)),
            _const_spec((F_WIDTH, d)), _const_spec((NA_WIDTH, d)), _const_spec((CONV_WIDTH, d)), _const_spec((d, d)),
        ],
        out_specs=row(d),
        out_shape=jax.ShapeDtypeStruct((n, d), F32),
        scratch_shapes=[pltpu.VMEM((tm, d), F32)],
        compiler_params=_params(("parallel",)),
        name="merge",
    )(fm, attn, zu, zb, zc, zu, zc, zu, zc, gates, x2, mod, conv_w, wf, wa, wc, wo)


def _ffn_kernel(x_ref, mod_ref, g_ref, wg_ref, wu_ref, wd_ref, fg_ref, o_ref, acc_ref, *, tf, final):
    x = x_ref[...]
    h = _norm_mod(x, g_ref[...], mod_ref[0, 3:4, :], mod_ref[0, 4:5, :]).astype(BF16)
    ff = wg_ref.shape[1]
    for f in range(0, ff, tf):
        g = _dot(h, wg_ref[:, f:f + tf])
        u = _dot(h, wu_ref[:, f:f + tf])
        a = (g * jax.nn.sigmoid(g) * u).astype(BF16)
        y = _dot(a, wd_ref[f:f + tf, :])
        if f == 0:
            acc_ref[...] = y
        else:
            acc_ref[...] += y
    out = x + mod_ref[0, 5:6, :] * acc_ref[...]
    o_ref[...] = _rmsnorm(out, fg_ref[...]) if final else out


def _ffn(x2, mod, g, wg, wu, wd, final_g, seq_len, final):
    n, d = x2.shape
    ff = wg.shape[1]
    tm = min(ROW_TILE, seq_len)
    tf = 256 if ff % 256 == 0 else LANES
    row = pl.BlockSpec((tm, d), lambda i: (i, 0))
    return pl.pallas_call(
        functools.partial(_ffn_kernel, tf=tf, final=final),
        grid=(n // tm,),
        in_specs=[
            row,
            pl.BlockSpec((1, N_MOD, d), lambda i: (i * tm // seq_len, 0, 0)),
            _const_spec((1, d)), _const_spec((d, ff)), _const_spec((d, ff)), _const_spec((ff, d)),
            _const_spec((1, d)),
        ],
        out_specs=row,
        out_shape=jax.ShapeDtypeStruct((n, d), F32),
        scratch_shapes=[pltpu.VMEM((tm, d), F32)],
        compiler_params=_params(("parallel",)),
        name="ffn",
    )(x2, mod, g.reshape(1, d), wg, wu, wd, final_g.reshape(1, d))


def _router_kernel(x_ref, mod_ref, g_ref, wr_ref, h_ref, r_ref, *, n_exp):
    h = _norm_mod(x_ref[...], g_ref[...], mod_ref[0, 3:4, :], mod_ref[0, 4:5, :])
    h_ref[...] = h.astype(BF16)
    logits = _dot_split(h, wr_ref[...])
    lane = lax.broadcasted_iota(I32, logits.shape, 1)
    lane_f = lane.astype(F32)
    neg = -jnp.inf
    l1 = jnp.where(lane < n_exp, logits, neg)
    m1 = l1.max(axis=-1, keepdims=True)
    i1 = jnp.where(l1 == m1, lane_f, float(LANES)).min(axis=-1, keepdims=True)
    l2 = jnp.where(lane_f == i1, neg, l1)
    m2 = l2.max(axis=-1, keepdims=True)
    i2 = jnp.where(l2 == m2, lane_f, float(LANES)).min(axis=-1, keepdims=True)
    e = jnp.exp(m2 - m1)
    w1 = 1.0 / (1.0 + e)
    w2 = e * w1
    out = jnp.where(lane == 0, i1, 0.0)
    out = jnp.where(lane == 1, i2, out)
    out = jnp.where(lane == 2, w1, out)
    out = jnp.where(lane == 3, w2, out)
    r_ref[...] = out


def _router(x2, mod, g, w_router, seq_len):
    n, d = x2.shape
    n_exp = w_router.shape[1]
    tm = min(ROW_TILE, seq_len)
    wr = jnp.zeros((d, LANES), F32).at[:, :n_exp].set(w_router)
    row = lambda w: pl.BlockSpec((tm, w), lambda i: (i, 0))
    return pl.pallas_call(
        functools.partial(_router_kernel, n_exp=n_exp),
        grid=(n // tm,),
        in_specs=[
            row(d),
            pl.BlockSpec((1, N_MOD, d), lambda i: (i * tm // seq_len, 0, 0)),
            _const_spec((1, d)), _const_spec((d, LANES)),
        ],
        out_specs=[row(d), row(LANES)],
        out_shape=[jax.ShapeDtypeStruct((n, d), BF16), jax.ShapeDtypeStruct((n, LANES), F32)],
        compiler_params=_params(("parallel",)),
        name="router",
    )(x2, mod, g.reshape(1, d), wr)


def _count_le(sorted_vals, v):
    return jnp.sum((sorted_vals[None, :] <= v[:, None]).astype(I32), axis=1)


def _moe_plan(route, n_exp):
    n = route.shape[0]
    te, tg, tc = MOE_EXPERT_TILE, MOE_SLOT_TILE, MOE_TOKEN_CHUNK
    assert n % MOE_GATHER_CHUNK == 0 and MOE_GATHER_CHUNK % tc == 0 and te % tg == 0
    idx = route[:, :TOP_K].astype(I32)
    wts = route[:, TOP_K:2 * TOP_K]
    ex = jnp.arange(n_exp, dtype=I32)
    hit = idx[:, :, None] == ex[None, None, :]
    sel = hit.any(axis=1)
    gate = jnp.sum(jnp.where(hit, wts[:, :, None], 0.0), axis=1)
    seli = sel.astype(I32)
    cnt = seli.sum(axis=0)
    padded = (cnt + te - 1) // te * te
    ends = jnp.cumsum(padded)
    off = ends - padded
    total = ends[-1]
    rank = jnp.cumsum(seli, axis=0) - seli
    pos = jnp.where(sel, off[None, :] + rank, -1)
    n_slots = TOP_K * n + n_exp * te
    n_tg, n_te, n_ch = n_slots // tg, n_slots // te, n // tc
    g_max = n_exp * n_ch + n_tg

    posc = pos.reshape(n_ch, tc, n_exp)
    hasc = (posc >= 0).any(axis=1)
    pmin = jnp.where(posc >= 0, posc, n_slots).min(axis=1)
    pmax = posc.max(axis=1)

    te_base = jnp.arange(n_te, dtype=I32) * te
    te_valid = (te_base < total).astype(I32)
    last_valid = jnp.maximum(total // te - 1, 0)
    te_exp = jnp.minimum(_count_le(ends, jnp.minimum(te_base, last_valid * te)), n_exp - 1).astype(I32)

    tg_base = jnp.arange(n_tg, dtype=I32) * tg
    tg_exp = jnp.minimum(_count_le(ends, tg_base), n_exp - 1).astype(I32)
    used_end = off[tg_exp] + cnt[tg_exp]
    has = tg_base < used_end
    ratio = MOE_GATHER_CHUNK // tc
    n_gc = n_ch // ratio
    has_g = hasc.reshape(n_gc, ratio, n_exp).any(axis=1)
    pmin_g = pmin.reshape(n_gc, ratio, n_exp).min(axis=1)
    pmax_g = pmax.reshape(n_gc, ratio, n_exp).max(axis=1)
    ch = jnp.arange(n_gc, dtype=I32)[None, :]
    has_e = jnp.take(has_g.T, tg_exp, axis=0)
    first_c = jnp.where(has_e & (jnp.take(pmax_g.T, tg_exp, axis=0) >= tg_base[:, None]), ch, n_gc).min(axis=1)
    last_c = jnp.where(has_e & (jnp.take(pmin_g.T, tg_exp, axis=0) < tg_base[:, None] + tg), ch, -1).max(axis=1)
    c_lo = jnp.where(has, first_c, 0)
    c_hi = jnp.where(has, last_c, 0)
    n_it = c_hi - c_lo + 1
    start = jnp.cumsum(n_it) - n_it
    tot_g = start[-1] + n_it[-1]
    gg = jnp.arange(n_exp * n_gc + n_tg, dtype=I32)
    gc = jnp.minimum(gg, tot_g - 1)
    s_of = (_count_le(start, gc) - 1).astype(I32)
    ga_valid = (gg < tot_g).astype(I32)
    ga_tile = s_of
    ga_chunk = (c_lo[s_of] + gc - start[s_of]).astype(I32)
    ga_first = ((gc == start[s_of]) & (gg < tot_g)).astype(I32)
    g = jnp.arange(g_max, dtype=I32)

    s_lo = jnp.where(hasc, pmin // tg, 0).reshape(-1)
    s_hi = jnp.where(hasc, pmax // tg, 0).reshape(-1)
    ccnt = jnp.where(hasc.reshape(-1), s_hi - s_lo + 1, 0)
    cend = jnp.cumsum(ccnt)
    cstart = cend - ccnt
    tot_c = cend[-1]
    hc = jnp.minimum(g, tot_c - 1)
    pair = (_count_le(cstart, hc) - 1).astype(I32)
    co_valid = (g < tot_c).astype(I32)
    co_chunk = (pair // n_exp).astype(I32)
    co_exp = (pair % n_exp).astype(I32)
    co_tile = (s_lo[pair] + hc - cstart[pair]).astype(I32)
    chunk_start = cstart.reshape(n_ch, n_exp)[:, 0]
    chunk_end = cend.reshape(n_ch, n_exp)[:, -1]
    co_first = ((hc == chunk_start[co_chunk]) & (g < tot_c)).astype(I32)
    co_last = ((hc == chunk_end[co_chunk] - 1) & (g < tot_c)).astype(I32)

    pos_t = pos.T.astype(I32)
    return dict(n_slots=n_slots, g_max=g_max, te_exp=te_exp, te_valid=te_valid, tg_exp=tg_exp,
                ga=(ga_tile, ga_chunk, ga_first, ga_valid), co=(co_tile, co_chunk, co_exp, co_first, co_last, co_valid),
                pos_t=pos_t, pos_f=pos.astype(F32), gate=gate)


def _gather_kernel(it_tile, it_chunk, it_first, it_valid, tg_exp, h_ref, pos_ref, o_ref):
    g = pl.program_id(0)
    tg, tc = MOE_SLOT_TILE, MOE_GATHER_CHUNK

    @pl.when(it_valid[g] == 1)
    def _():
        s = it_tile[g]
        e = tg_exp[s]
        pos_row = pos_ref[pl.ds(e, 1), :]
        slot = s * tg + lax.broadcasted_iota(I32, (tg, tc), 0)
        onehot = jnp.where(pos_row == slot, 1.0, 0.0).astype(BF16)
        r = _dot(onehot, h_ref[...]).astype(BF16)

        @pl.when(it_first[g] == 1)
        def _():
            o_ref[...] = r

        @pl.when(it_first[g] == 0)
        def _():
            o_ref[...] += r


def _moe_gather(h_bf, plan):
    n, d = h_bf.shape
    n_exp = plan["pos_t"].shape[0]
    tg, tc = MOE_SLOT_TILE, MOE_GATHER_CHUNK
    grid_spec = pltpu.PrefetchScalarGridSpec(
        num_scalar_prefetch=5,
        grid=(plan["ga"][0].shape[0],),
        in_specs=[
            pl.BlockSpec((tc, d), lambda g, t, c, f, v, te: (c[g], 0)),
            pl.BlockSpec((n_exp, tc), lambda g, t, c, f, v, te: (0, c[g])),
        ],
        out_specs=pl.BlockSpec((tg, d), lambda g, t, c, f, v, te: (t[g], 0)),
    )
    return pl.pallas_call(
        _gather_kernel,
        grid_spec=grid_spec,
        out_shape=jax.ShapeDtypeStruct((plan["n_slots"], d), BF16),
        compiler_params=_params(("arbitrary",)),
        name="moe_gather",
    )(*plan["ga"], plan["tg_exp"], h_bf, plan["pos_t"])


def _experts_kernel(te_exp, te_valid, h_ref, wg_ref, wu_ref, wd_ref, y_ref, acc_ref, *, sub):
    i = pl.program_id(0)
    f = pl.program_id(1)

    @pl.when(f == 0)
    def _():
        acc_ref[...] = jnp.zeros_like(acc_ref)

    @pl.when(te_valid[i] == 1)
    def _():
        h = h_ref[...]
        for c in range(0, wg_ref.shape[2], sub):
            g = _dot(h, wg_ref[0, :, c:c + sub])
            u = _dot(h, wu_ref[0, :, c:c + sub])
            a = (g * jax.nn.sigmoid(g) * u).astype(BF16)
            acc_ref[...] += _dot(a, wd_ref[0, c:c + sub, :])

    @pl.when(f == pl.num_programs(1) - 1)
    def _():
        y_ref[...] = acc_ref[...].astype(BF16)


def _moe_experts(hs, wg, wu, wd, plan):
    n_slots, d = hs.shape
    ff = wg.shape[2]
    te = MOE_EXPERT_TILE
    sub = MOE_FF_CHUNK if ff % (2 * MOE_FF_CHUNK) == 0 else LANES
    tf = ff // 2 if (ff // 2) % sub == 0 else ff
    grid_spec = pltpu.PrefetchScalarGridSpec(
        num_scalar_prefetch=2,
        grid=(n_slots // te, ff // tf),
        in_specs=[
            pl.BlockSpec((te, d), lambda i, f, e, v: (i, 0)),
            pl.BlockSpec((1, d, tf), lambda i, f, e, v: (e[i], 0, f * v[i])),
            pl.BlockSpec((1, d, tf), lambda i, f, e, v: (e[i], 0, f * v[i])),
            pl.BlockSpec((1, tf, d), lambda i, f, e, v: (e[i], f * v[i], 0)),
        ],
        out_specs=pl.BlockSpec((te, d), lambda i, f, e, v: (i, 0)),
        scratch_shapes=[pltpu.VMEM((te, d), F32)],
    )
    return pl.pallas_call(
        functools.partial(_experts_kernel, sub=sub),
        grid_spec=grid_spec,
        out_shape=jax.ShapeDtypeStruct((n_slots, d), BF16),
        compiler_params=_params(("arbitrary", "arbitrary")),
        name="moe_experts",
    )(plan["te_exp"], plan["te_valid"], hs, wg, wu, wd)


def _combine_kernel(it_tile, it_chunk, it_exp, it_first, it_last, it_valid,
                    y_ref, pos_ref, gate_ref, x_ref, mod_ref, fg_ref, o_ref, acc_ref, *, final):
    g = pl.program_id(0)
    tg, tc = MOE_SLOT_TILE, MOE_TOKEN_CHUNK

    @pl.when(it_first[g] == 1)
    def _():
        acc_ref[...] = jnp.zeros_like(acc_ref)

    @pl.when(it_valid[g] == 1)
    def _():
        e = it_exp[g]
        lane = lax.broadcasted_iota(I32, pos_ref.shape, 1)
        pos_col = jnp.sum(jnp.where(lane == e, pos_ref[...], 0.0), axis=-1, keepdims=True)
        w_col = jnp.sum(jnp.where(lane == e, gate_ref[...], 0.0), axis=-1, keepdims=True)
        slot = (it_tile[g] * tg + lax.broadcasted_iota(I32, (tc, tg), 1)).astype(F32)
        onehot = jnp.where(pos_col == slot, 1.0, 0.0).astype(BF16)
        acc_ref[...] += w_col * _dot(onehot, y_ref[...])

    @pl.when(it_last[g] == 1)
    def _():
        out = x_ref[...] + mod_ref[0, 5:6, :] * acc_ref[...]
        o_ref[...] = _rmsnorm(out, fg_ref[...]) if final else out


def _moe_combine(y, x2, mod, final_g, plan, seq_len, final):
    n, d = x2.shape
    n_exp = plan["pos_t"].shape[0]
    tg, tc = MOE_SLOT_TILE, MOE_TOKEN_CHUNK
    assert seq_len % tc == 0
    grid_spec = pltpu.PrefetchScalarGridSpec(
        num_scalar_prefetch=6,
        grid=(plan["g_max"],),
        in_specs=[
            pl.BlockSpec((tg, d), lambda g, t, c, *_: (t[g], 0)),
            pl.BlockSpec((tc, n_exp), lambda g, t, c, *_: (c[g], 0)),
            pl.BlockSpec((tc, n_exp), lambda g, t, c, *_: (c[g], 0)),
            pl.BlockSpec((tc, d), lambda g, t, c, *_: (c[g], 0)),
            pl.BlockSpec((1, N_MOD, d), lambda g, t, c, *_: (c[g] * tc // seq_len, 0, 0)),
            pl.BlockSpec((1, d), lambda g, *_: (0, 0)),
        ],
        out_specs=pl.BlockSpec((tc, d), lambda g, t, c, *_: (c[g], 0)),
        scratch_shapes=[pltpu.VMEM((tc, d), F32)],
    )
    return pl.pallas_call(
        functools.partial(_combine_kernel, final=final),
        grid_spec=grid_spec,
        out_shape=jax.ShapeDtypeStruct((n, d), F32),
        compiler_params=_params(("arbitrary",)),
        name="moe_combine",
    )(*plan["co"], y, plan["pos_f"], plan["gate"], x2, mod, final_g.reshape(1, d))


def _token_mix(x2, mod, l, seq_len, bt, p, kv_ctx, tabs, bias):
    a0, b0, q, k, v, zu, zb, zc, gates = _inproj(x2, mod, p["norm1_g"][l], p["w_in"][l], tabs[3], seq_len)
    fm = _fourier(a0, b0, bt, seq_len, *tabs[:3])
    if kv_ctx is None:
        attn = _ctxattn(q, k, v, bt, seq_len)
    else:
        attn = _natten(q, k, v, kv_ctx[0], kv_ctx[1], bias, bt, seq_len, kv_ctx[0].shape[0] // bt)
    out = _merge(fm, attn, zu, zb, zc, gates, x2, mod, p["conv_w"][l], p["w_fourier"][l], p["w_na"][l],
                 p["w_conv_out"][l], p["w_out"][l], seq_len)
    return out, (k, v)


def kernel(x, c, ctx, c_ctx, norm1_g, norm2_g, w_ada, b_ada, w_in, conv_w, na_rpb, w_fourier, w_na, w_conv_out, w_out, ffn_w_gate, ffn_w_up, ffn_w_down, moe_router, moe_w_gate, moe_w_up, moe_w_down, final_g):
    bt, seq, d = x.shape
    lc = ctx.shape[1]
    depth = w_in.shape[0]
    n_exp = moe_router.shape[-1]
    bf = lambda w: w.astype(BF16)
    p = dict(norm1_g=norm1_g, w_in=bf(w_in), conv_w=conv_w, w_fourier=bf(w_fourier), w_na=bf(w_na),
             w_conv_out=bf(w_conv_out), w_out=bf(w_out))
    ffn_g, ffn_u, ffn_d = bf(ffn_w_gate), bf(ffn_w_up), bf(ffn_w_down)
    moe_g, moe_u, moe_d = bf(moe_w_gate), bf(moe_w_up), bf(moe_w_down)

    rows8 = -(-(bt + 1) // 8) * 8
    cvec = jnp.zeros((rows8, d), F32).at[:bt].set(c).at[bt].set(c_ctx)
    mods = _adaln(cvec, w_ada, b_ada)

    tabs_x = _fft_tables(seq)
    tabs_c = _fft_tables(lc)
    x2 = x.reshape(bt * seq, d)
    c2 = ctx.reshape(bt * lc, d)
    for l in range(depth):
        last = l == depth - 1
        mod_x = mods[l, :bt].reshape(bt, N_MOD, d)
        mod_c = jnp.broadcast_to(mods[l, bt].reshape(1, N_MOD, d), (bt, N_MOD, d))
        bias = _bias_tables(na_rpb[l], seq // GRID_W)
        dense = l % 2 == 0
        j = l // 2
        if last:
            outs = _inproj(c2, mod_c, norm1_g[l], p["w_in"][l], tabs_c[3], lc)
            kv_ctx = (outs[3], outs[4])
        else:
            c_mid, kv_ctx = _token_mix(c2, mod_c, l, lc, bt, p, None, tabs_c, None)
            if not dense:
                raise NotImplementedError("MoE channel mixer on context tokens (only reached when depth > 2)")
            c2 = _ffn(c_mid, mod_c, norm2_g[l], ffn_g[j], ffn_u[j], ffn_d[j], final_g, lc, False)
        x_mid, _ = _token_mix(x2, mod_x, l, seq, bt, p, kv_ctx, tabs_x, bias)
        if dense:
            x2 = _ffn(x_mid, mod_x, norm2_g[l], ffn_g[j], ffn_u[j], ffn_d[j], final_g, seq, last)
        else:
            h_bf, route = _router(x_mid, mod_x, norm2_g[l], moe_router[j], seq)
            plan = _moe_plan(route, n_exp)
            hs = _moe_gather(h_bf, plan)
            y = _moe_experts(hs, moe_g[j], moe_u[j], moe_d[j], plan)
            x2 = _moe_combine(y, x_mid, mod_x, final_g, plan, seq, last)
    return x2.reshape(bt, seq, d)
```
